```python
import math
import jax, jax.numpy as jnp
from jax import lax
import numpy as np

D_MODEL = 1024
BATCH = 1
SEQ = 16384
DEPTH = 1
DEC_BATCH = 128
DEC_SEQ = 4
PAST_LEN = 8192
PAGE_SIZE = 128

D_CONV = D_MODEL // 2
CONV_WIDTH = 31
HEAD_DIM = 64
D_ATT = D_MODEL // 2
N_HEADS = D_ATT // HEAD_DIM
QBLK = 128
N_BRANCH = 2
IN_COLS = 2 * D_CONV + 3 * D_ATT + N_BRANCH * D_MODEL
N_KEYS = 128
N_EXPERTS = N_KEYS * N_KEYS
PEER_HEADS = 8
D_KEY = 256
TOPK = 16
TOK_BLK = 128
EPS = 1e-6
SB_BIAS_INIT = -6.0

N_PAGES = PAST_LEN // PAGE_SIZE

kernel_name = "hybrid_conv_stickbreak_peer_step"


def rmsnorm(x, g):
    x32 = x.astype(jnp.float32)
    y = x32 * lax.rsqrt(jnp.mean(x32 * x32, axis=-1, keepdims=True) + EPS)
    return (y * g.astype(jnp.float32)).astype(x.dtype)


def layernorm(x, g, b):
    x32 = x.astype(jnp.float32)
    mu = jnp.mean(x32, axis=-1, keepdims=True)
    var = jnp.mean(jnp.square(x32 - mu), axis=-1, keepdims=True)
    y = (x32 - mu) * lax.rsqrt(var + EPS)
    return (y * g.astype(jnp.float32) + b.astype(jnp.float32)).astype(x.dtype)


def conv_branch(cin, buf, dw_w, dw_b, ln_g, ln_b, w_co):
    a, b = jnp.split(cin, 2, axis=-1)
    u = a * jax.nn.sigmoid(b)
    ext = jnp.concatenate([buf.astype(u.dtype), u], axis=1)
    y = lax.conv_general_dilated(
        ext, dw_w[:, None, :].astype(u.dtype), window_strides=(1,), padding='VALID',
        dimension_numbers=('NWC', 'WIO', 'NWC'), feature_group_count=D_CONV) + dw_b
    y = layernorm(y, ln_g, ln_b)
    y = jax.nn.silu(y)
    return y @ w_co, ext[:, -(CONV_WIDTH - 1):]


def stick_breaking(q, k, v, bias, q_pos, k_pos):
    z = jnp.einsum('bqhd,bkhd->bhqk', q.astype(jnp.float32), k.astype(jnp.float32)) * (HEAD_DIM ** -0.5)
    z = z + bias.astype(jnp.float32)[None, :, None, None]
    mask = k_pos[None, :] < q_pos[:, None]
    m = jnp.where(mask, jax.nn.log_sigmoid(-z), 0.0)
    later = lax.cumsum(m, axis=3, reverse=True) - m
    a = jnp.where(mask, jnp.exp(jax.nn.log_sigmoid(z) + later), 0.0)
    return jnp.einsum('bhqk,bkhd->bqhd', a, v.astype(jnp.float32)).astype(v.dtype)


def sb_prompt(q, k, v, bias):
    b, s, h, d = q.shape
    k_pos = jnp.arange(s)

    def blk(i):
        qb = lax.dynamic_slice_in_dim(q, i * QBLK, QBLK, axis=1)
        return stick_breaking(qb, k, v, bias, i * QBLK + jnp.arange(QBLK), k_pos)

    out = lax.map(blk, jnp.arange(s // QBLK))
    return jnp.transpose(out, (1, 0, 2, 3, 4)).reshape(b, s, h, d)


def peer_block(hb, wq, keys, u_tab, v_tab):
    t = hb.shape[0]
    q = (hb @ wq).reshape(t, PEER_HEADS, D_KEY)
    q1, q2 = q[..., :D_KEY // 2], q[..., D_KEY // 2:]
    s1 = jnp.einsum('thd,hnd->thn', q1, keys[0])
    s2 = jnp.einsum('thd,hnd->thn', q2, keys[1])
    v1, i1 = lax.top_k(s1, TOPK)
    v2, i2 = lax.top_k(s2, TOPK)
    cand = (v1[..., :, None] + v2[..., None, :]).reshape(t, PEER_HEADS, TOPK * TOPK)
    sc, pos = lax.top_k(cand, TOPK)
    e1 = jnp.take_along_axis(i1, pos // TOPK, axis=-1)
    e2 = jnp.take_along_axis(i2, pos % TOPK, axis=-1)
    eid = e1 * N_KEYS + e2
    g = jax.nn.softmax(sc.astype(jnp.float32), axis=-1)
    act = jax.nn.gelu(jnp.einsum('td,thkd->thk', hb, u_tab[eid]).astype(jnp.float32))
    return jnp.einsum('thk,thkd->td', (g * act).astype(hb.dtype), v_tab[eid])


def peer(h, wq, keys, u_tab, v_tab):
    n_b, n_t, d = h.shape
    flat = h.reshape(n_b * n_t, d)
    n = flat.shape[0]
    nblk = -(-n // TOK_BLK)
    padded = jnp.pad(flat, ((0, nblk * TOK_BLK - n), (0, 0))).reshape(nblk, TOK_BLK, d)
    out = lax.map(lambda hb: peer_block(hb, wq, keys, u_tab, v_tab), padded)
    return out.reshape(nblk * TOK_BLK, d)[:n].reshape(n_b, n_t, d)


def layer_forward(x, conv_buf, attend, norm1_g, w_in, dw_w, dw_b, ln_g, ln_b, w_co, w_ao, sb_bias,
                  w_o, norm2_g, wq, keys, u_tab, v_tab):
    nb, nt, _ = x.shape
    h = rmsnorm(x, norm1_g)
    p = h @ w_in
    c0 = 2 * D_CONV
    cin = p[..., :c0]
    q = p[..., c0:c0 + D_ATT].reshape(nb, nt, N_HEADS, HEAD_DIM)
    k = p[..., c0 + D_ATT:c0 + 2 * D_ATT].reshape(nb, nt, N_HEADS, HEAD_DIM)
    v = p[..., c0 + 2 * D_ATT:c0 + 3 * D_ATT].reshape(nb, nt, N_HEADS, HEAD_DIM)
    gates = jax.nn.sigmoid(p[..., c0 + 3 * D_ATT:].reshape(nb, nt, N_BRANCH, D_MODEL))
    y_conv, new_buf = conv_branch(cin, conv_buf, dw_w, dw_b, ln_g, ln_b, w_co)
    y_att = attend(q, k, v, sb_bias).reshape(nb, nt, D_ATT) @ w_ao
    merged = gates[:, :, 0] * y_conv + gates[:, :, 1] * y_att
    x = x + merged @ w_o
    x = x + peer(rmsnorm(x, norm2_g), wq, keys, u_tab, v_tab)
    return x, k, v, new_buf


def setup_inputs(seed: int = 0) -> dict:
    key = jax.random.key(seed)
    ks = jax.random.split(key, 24)

    def nrm(k, shape, scale):
        return jax.random.normal(k, shape, jnp.float32) * scale

    n_pool = (DEC_BATCH * N_PAGES * 5) // 4
    perm = jax.random.permutation(ks[5], n_pool)[:DEC_BATCH * N_PAGES]
    page_table = perm.reshape(DEC_BATCH, N_PAGES).astype(jnp.int32)
    return {
        "x_prompt": nrm(ks[0], (BATCH, SEQ, D_MODEL), 1.0),
        "x_sample": nrm(ks[1], (DEC_BATCH, DEC_SEQ, D_MODEL), 1.0),
        "cache_k": nrm(ks[2], (DEPTH, n_pool, PAGE_SIZE, N_HEADS, HEAD_DIM), 1.0),
        "cache_v": nrm(ks[3], (DEPTH, n_pool, PAGE_SIZE, N_HEADS, HEAD_DIM), 1.0),
        "state_conv": nrm(ks[4], (DEPTH, DEC_BATCH, CONV_WIDTH - 1, D_CONV), 0.5),
        "page_table": page_table,
        "norm1_g": 1.0 + nrm(ks[6], (DEPTH, D_MODEL), 0.02),
        "w_in": nrm(ks[7], (DEPTH, D_MODEL, IN_COLS), D_MODEL ** -0.5),
        "dw_w": nrm(ks[8], (DEPTH, CONV_WIDTH, D_CONV), CONV_WIDTH ** -0.5),
        "dw_b": nrm(ks[9], (DEPTH, D_CONV), 0.02),
        "conv_ln_g": 1.0 + nrm(ks[10], (DEPTH, D_CONV), 0.02),
        "conv_ln_b": nrm(ks[11], (DEPTH, D_CONV), 0.02),
        "w_conv_out": nrm(ks[12], (DEPTH, D_CONV, D_MODEL), D_CONV ** -0.5),
        "w_att_out": nrm(ks[13], (DEPTH, D_ATT, D_MODEL), D_ATT ** -0.5),
        "sb_bias": SB_BIAS_INIT + nrm(ks[21], (DEPTH, N_HEADS), 0.1),
        "w_out": nrm(ks[14], (DEPTH, D_MODEL, D_MODEL), D_MODEL ** -0.5),
        "norm2_g": 1.0 + nrm(ks[15], (DEPTH, D_MODEL), 0.02),
        "peer_wq": nrm(ks[16], (DEPTH, D_MODEL, PEER_HEADS * D_KEY), D_MODEL ** -0.5),
        "peer_keys": nrm(ks[17], (DEPTH, 2, PEER_HEADS, N_KEYS, D_KEY // 2), (D_KEY // 2) ** -0.5),
        "peer_u": nrm(ks[18], (DEPTH, N_EXPERTS, D_MODEL), D_MODEL ** -0.5),
        "peer_v": nrm(ks[19], (DEPTH, N_EXPERTS, D_MODEL), 0.2),
        "final_g": 1.0 + nrm(ks[20], (D_MODEL,), 0.02),
    }


def reference(x_prompt, x_sample, cache_k, cache_v, state_conv, page_table, norm1_g, w_in, dw_w,
              dw_b, conv_ln_g, conv_ln_b, w_conv_out, w_att_out, sb_bias, w_out, norm2_g, peer_wq,
              peer_keys, peer_u, peer_v, final_g):
    xp, xs = x_prompt, x_sample
    n_dec, t_dec = xs.shape[0], xs.shape[1]
    n_pages = page_table.shape[1]
    past_len = n_pages * PAGE_SIZE
    q_pos_s = past_len + jnp.arange(t_dec)
    k_pos_s = jnp.arange(past_len + t_dec)
    kp_l, vp_l, cp_l, ks_l, vs_l, cs_l = [], [], [], [], [], []
    for l in range(DEPTH):
        weights = (norm1_g[l], w_in[l], dw_w[l], dw_b[l], conv_ln_g[l], conv_ln_b[l],
                   w_conv_out[l], w_att_out[l], sb_bias[l], w_out[l], norm2_g[l], peer_wq[l],
                   peer_keys[l], peer_u[l], peer_v[l])
        zero_buf = jnp.zeros((xp.shape[0], CONV_WIDTH - 1, D_CONV), xp.dtype)
        xp, kp, vp, bp = layer_forward(xp, zero_buf, sb_prompt, *weights)

        k_past = cache_k[l][page_table].reshape(n_dec, past_len, N_HEADS, HEAD_DIM)
        v_past = cache_v[l][page_table].reshape(n_dec, past_len, N_HEADS, HEAD_DIM)

        def attend_sample(q, k, v, bias, k_past=k_past, v_past=v_past):
            k_all = jnp.concatenate([k_past.astype(k.dtype), k], axis=1)
            v_all = jnp.concatenate([v_past.astype(v.dtype), v], axis=1)
            return stick_breaking(q, k_all, v_all, bias, q_pos_s, k_pos_s)

        xs, ksm, vsm, bs = layer_forward(xs, state_conv[l], attend_sample, *weights)
        kp_l.append(kp); vp_l.append(vp); cp_l.append(bp)
        ks_l.append(ksm); vs_l.append(vsm); cs_l.append(bs)
    y_prompt = rmsnorm(xp, final_g)
    y_sample = rmsnorm(xs, final_g)
    return (y_prompt, y_sample, jnp.stack(kp_l), jnp.stack(vp_l), jnp.stack(cp_l),
            jnp.stack(ks_l), jnp.stack(vs_l), jnp.stack(cs_l))
```

```python
import functools

import jax
import jax.numpy as jnp
from jax import lax
from jax.experimental import pallas as pl
from jax.experimental.pallas import tpu as pltpu

EPS = 1e-6
TOPK = 16
F32 = jnp.float32
BF16 = jnp.bfloat16
LANES = 128
SUBLANES = 8
VMEM_LIMIT = 56 * 1024 * 1024

_NT = (((1,), (1,)), ((), ()))


def _params(n_axes):
    return pltpu.CompilerParams(dimension_semantics=("arbitrary",) * n_axes,
                                vmem_limit_bytes=VMEM_LIMIT)


def _sigmoid(x):
    return 1.0 / (1.0 + jnp.exp(-x))


def _softplus(z):
    return jnp.maximum(z, 0.0) + jnp.log(1.0 + jnp.exp(-jnp.abs(z)))


def _rms(x, g):
    ms = jnp.mean(x * x, axis=-1, keepdims=True)
    return (x * lax.rsqrt(ms + EPS)) * g


def _dot(a, b):
    return jnp.dot(a, b, preferred_element_type=F32)


def _split_dot(x, tri):
    hi = x.astype(BF16)
    lo = (x - hi.astype(F32)).astype(BF16)
    return _dot(hi, tri) + _dot(lo, tri)


def _inproj_kernel(x_ref, g_ref, w_ref, u_ref, k_ref, v_ref, qb_ref, kb_ref, vb_ref, g0_ref, g1_ref,
                   *, dc, da, dm, qscale):
    h = _rms(x_ref[...], g_ref[...]).astype(BF16)

    def proj(c0, n):
        return _dot(h, w_ref[:, c0:c0 + n])

    u_ref[...] = proj(0, dc) * _sigmoid(proj(dc, dc))
    c0 = 2 * dc
    qb_ref[...] = (proj(c0, da) * qscale).astype(BF16)
    k = proj(c0 + da, da)
    k_ref[...] = k
    kb_ref[...] = k.astype(BF16)
    v = proj(c0 + 2 * da, da)
    v_ref[...] = v
    vb_ref[...] = v.astype(BF16)
    g0_ref[...] = _sigmoid(proj(c0 + 3 * da, dm))
    g1_ref[...] = _sigmoid(proj(c0 + 3 * da + dm, dm))


def _inproj(x, g, w_bf, dc, da, qscale):
    t, dm = x.shape
    tb = min(256, t)
    assert t % tb == 0
    row = lambda n: pl.BlockSpec((tb, n), lambda i: (i, 0))
    full = lambda a: pl.BlockSpec(a.shape, lambda i: (0,) * a.ndim)
    g2 = g.reshape(1, dm)
    return pl.pallas_call(
        functools.partial(_inproj_kernel, dc=dc, da=da, dm=dm, qscale=qscale),
        grid=(t // tb,),
        in_specs=[row(dm), full(g2), full(w_bf)],
        out_specs=[row(dc), row(da), row(da), row(da), row(da), row(da), row(dm), row(dm)],
        out_shape=[jax.ShapeDtypeStruct((t, dc), F32), jax.ShapeDtypeStruct((t, da), F32),
                   jax.ShapeDtypeStruct((t, da), F32), jax.ShapeDtypeStruct((t, da), BF16),
                   jax.ShapeDtypeStruct((t, da), BF16), jax.ShapeDtypeStruct((t, da), BF16),
                   jax.ShapeDtypeStruct((t, dm), F32), jax.ShapeDtypeStruct((t, dm), F32)],
        compiler_params=_params(1), name="inproj",
    )(x, g2, w_bf)


def _ln_silu(acc, lg, lb):
    mu = jnp.mean(acc, axis=-1, keepdims=True)
    d = acc - mu
    var = jnp.mean(d * d, axis=-1, keepdims=True)
    y = d * lax.rsqrt(var + EPS) * lg + lb
    return y * _sigmoid(y)


def _conv_prompt_kernel(u_ref, halo_ref, dw_ref, db_ref, lg_ref, lb_ref, o_ref, ext_ref,
                        *, tb, halo, width, chunk):
    first = pl.program_id(0) == 0
    ext_ref[0:halo, :] = jnp.where(first, 0.0, halo_ref[...])
    ext_ref[halo:halo + tb, :] = u_ref[...]
    off = halo - (width - 1)
    for r in range(0, tb, chunk):
        acc = jnp.zeros((chunk, u_ref.shape[1]), F32) + db_ref[...]
        for w in range(width):
            acc = acc + dw_ref[w:w + 1, :] * ext_ref[r + off + w:r + off + w + chunk, :]
        o_ref[r:r + chunk, :] = _ln_silu(acc, lg_ref[...], lb_ref[...]).astype(BF16)


def _conv_prompt(u, dw, db, lg, lb):
    t, dc = u.shape
    width = dw.shape[0]
    halo = -(-(width - 1) // SUBLANES) * SUBLANES
    tb = min(256, t)
    chunk = min(32, tb)
    assert t % tb == 0 and tb % halo == 0 and tb % chunk == 0
    per = tb // halo
    vec = lambda a: pl.BlockSpec((1, dc), lambda i: (0, 0))
    return pl.pallas_call(
        functools.partial(_conv_prompt_kernel, tb=tb, halo=halo, width=width, chunk=chunk),
        grid=(t // tb,),
        in_specs=[pl.BlockSpec((tb, dc), lambda i: (i, 0)),
                  pl.BlockSpec((halo, dc), lambda i: (jnp.maximum(i * per - 1, 0), 0)),
                  pl.BlockSpec((width, dc), lambda i: (0, 0)), vec(db), vec(lg), vec(lb)],
        out_specs=pl.BlockSpec((tb, dc), lambda i: (i, 0)),
        out_shape=jax.ShapeDtypeStruct((t, dc), BF16),
        scratch_shapes=[pltpu.VMEM((halo + tb, dc), F32)],
        compiler_params=_params(1), name="conv_prompt",
    )(u, u, dw, db.reshape(1, dc), lg.reshape(1, dc), lb.reshape(1, dc))


def _conv_sample_kernel(ext_ref, dw_ref, db_ref, lg_ref, lb_ref, o_ref, *, width, tn):
    for t in range(tn):
        acc = jnp.zeros(ext_ref.shape[1:], F32) + db_ref[...]
        for w in range(width):
            acc = acc + dw_ref[w:w + 1, :] * ext_ref[t + w]
        o_ref[t] = _ln_silu(acc, lg_ref[...], lb_ref[...]).astype(BF16)


def _conv_sample(ext_t, dw, db, lg, lb, tn):
    rows, nb, dc = ext_t.shape
    width = dw.shape[0]
    assert rows == width - 1 + tn
    vec = lambda a: pl.BlockSpec((1, dc), lambda i: (0, 0))
    return pl.pallas_call(
        functools.partial(_conv_sample_kernel, width=width, tn=tn),
        grid=(1,),
        in_specs=[pl.BlockSpec((rows, nb, dc), lambda i: (0, 0, 0)),
                  pl.BlockSpec((width, dc), lambda i: (0, 0)), vec(db), vec(lg), vec(lb)],
        out_specs=pl.BlockSpec((tn, nb, dc), lambda i: (0, 0, 0)),
        out_shape=jax.ShapeDtypeStruct((tn, nb, dc), BF16),
        compiler_params=_params(1), name="conv_sample",
    )(ext_t, dw, db.reshape(1, dc), lg.reshape(1, dc), lb.reshape(1, dc))


def _sb_prompt_kernel(bias_ref, q_ref, k_ref, v_ref, o_ref, acc_ref, *, tq, hd):
    p = pl.program_id(0)
    i = pl.program_id(1)
    lane = lax.broadcasted_iota(jnp.int32, (tq, LANES), 1)
    q = q_ref[...]
    zero = jnp.zeros_like(q)
    qh = (jnp.where(lane < hd, q, zero), jnp.where(lane >= hd, q, zero))
    bh = (bias_ref[2 * p], bias_ref[2 * p + 1])
    r = lax.broadcasted_iota(jnp.int32, (tq, tq), 0)
    c = lax.broadcasted_iota(jnp.int32, (tq, tq), 1)
    tri = (r >= c).astype(BF16)
    causal = c < r
    acc_ref[...] = jnp.zeros_like(acc_ref)

    def block(j, carries, masked):
        start = pl.multiple_of(j * tq, tq)
        kb = k_ref[pl.ds(start, tq), :]
        vb = v_ref[pl.ds(start, tq), :]
        out = []
        for hh in range(2):
            z = lax.dot_general(qh[hh], kb, _NT, preferred_element_type=F32) + bh[hh]
            sp = _softplus(z)
            if masked:
                sp = jnp.where(causal, sp, 0.0)
            cs = _split_dot(sp, tri)
            a = jnp.exp(z - cs - carries[hh])
            if masked:
                a = jnp.where(causal, a, 0.0)
            acc_ref[hh] += _dot(a.astype(BF16), vb)
            out.append(carries[hh] + cs[:, 0:1])
        return tuple(out)

    c0 = jnp.zeros((tq, 1), F32)
    carries = block(i, (c0, c0), True)
    lax.fori_loop(0, i, lambda it, cr: block(i - 1 - it, cr, False), carries)
    o_ref[...] = jnp.where(lane < hd, acc_ref[0], acc_ref[1]).astype(BF16)


def _sb_prompt(qb, kb, vb, bias, hd):
    t, da = qb.shape
    assert 2 * hd == LANES and da % LANES == 0
    tq = min(256, t)
    assert t % tq == 0
    return pl.pallas_call(
        functools.partial(_sb_prompt_kernel, tq=tq, hd=hd),
        grid=(da // LANES, t // tq),
        in_specs=[pl.BlockSpec(memory_space=pltpu.SMEM),
                  pl.BlockSpec((tq, LANES), lambda p, i: (i, p)),
                  pl.BlockSpec((t, LANES), lambda p, i: (0, p)),
                  pl.BlockSpec((t, LANES), lambda p, i: (0, p))],
        out_specs=pl.BlockSpec((tq, LANES), lambda p, i: (i, p)),
        out_shape=jax.ShapeDtypeStruct((t, da), BF16),
        scratch_shapes=[pltpu.VMEM((2, tq, LANES), F32)],
        compiler_params=_params(2), name="sb_prompt",
    )(bias, qb, kb, vb)


def _sb_sample_kernel(pt_ref, q_ref, bias_ref, kn_ref, vn_ref, *rest, g, tn, nh, hd):
    kp, vp = rest[:g], rest[g:2 * g]
    o_ref, acc_ref, carry_ref, kpad_ref, vpad_ref = rest[2 * g:]
    js = pl.program_id(1)
    rows, dk = acc_ref.shape
    page = kpad_ref.shape[0]
    rr = lax.broadcasted_iota(jnp.int32, (rows, dk), 0)
    cc = lax.broadcasted_iota(jnp.int32, (rows, dk), 1)
    own = (cc // hd) == (rr % nh)
    qbd = jnp.where(own, q_ref[0], jnp.zeros((rows, dk), BF16))
    r = lax.broadcasted_iota(jnp.int32, (page, page), 0)
    c = lax.broadcasted_iota(jnp.int32, (page, page), 1)
    tri = (r >= c).astype(BF16)

    def block(k32, v32, keymask):
        z = lax.dot_general(qbd, k32.astype(BF16), _NT, preferred_element_type=F32) + bias_ref[...]
        sp = _softplus(z)
        if keymask is not None:
            sp = jnp.where(keymask, sp, 0.0)
        cs = _split_dot(sp, tri)
        a = jnp.exp(z - cs - carry_ref[...])
        if keymask is not None:
            a = jnp.where(keymask, a, 0.0)
        acc_ref[...] += _dot(a.astype(BF16), v32.astype(BF16))
        carry_ref[...] += cs[:, 0:1]

    @pl.when(js == 0)
    def _():
        acc_ref[...] = jnp.zeros_like(acc_ref)
        carry_ref[...] = jnp.zeros_like(carry_ref)
        kpad_ref[...] = jnp.zeros_like(kpad_ref)
        vpad_ref[...] = jnp.zeros_like(vpad_ref)
        kpad_ref[0:kn_ref.shape[1], :] = kn_ref[0]
        vpad_ref[0:vn_ref.shape[1], :] = vn_ref[0]
        qt = lax.broadcasted_iota(jnp.int32, (rows, page), 0) // nh
        ks = lax.broadcasted_iota(jnp.int32, (rows, page), 1)
        block(kpad_ref[...], vpad_ref[...], ks < qt)

    @pl.when(js > 0)
    def _():
        for gg in range(g):
            block(kp[gg][0], vp[gg][0], None)

    @pl.when(js == pl.num_programs(1) - 1)
    def _():
        m = jnp.where(own, acc_ref[...], 0.0)
        o_ref[0] = jnp.sum(m.reshape(tn, nh, dk), axis=1).astype(BF16)


def _sb_sample(q_rep, bias_rows, k_new, v_new, cache_k, cache_v, page_table, tn, nh, hd, g):
    nb, rows, dk = q_rep.shape
    npool, page, _ = cache_k.shape
    npages = page_table.shape[1]
    assert npages % g == 0 and rows == tn * nh and nh == SUBLANES
    pad = k_new.shape[1]

    def page_spec(gg):
        def imap(n, js, pt):
            lp = npages - 1 - ((jnp.maximum(js, 1) - 1) * g + gg)
            return (pt[n, lp], 0, 0)
        return pl.BlockSpec((1, page, dk), imap)

    seq = lambda shape: pl.BlockSpec((1,) + shape, lambda n, js, pt: (n, 0, 0))
    grid_spec = pltpu.PrefetchScalarGridSpec(
        num_scalar_prefetch=1,
        grid=(nb, 1 + npages // g),
        in_specs=[seq((rows, dk)), pl.BlockSpec((rows, page), lambda n, js, pt: (0, 0)),
                  seq((pad, dk)), seq((pad, dk))]
                 + [page_spec(gg) for gg in range(g)] * 2,
        out_specs=seq((tn, dk)),
        scratch_shapes=[pltpu.VMEM((rows, dk), F32), pltpu.VMEM((rows, 1), F32),
                        pltpu.VMEM((page, dk), F32), pltpu.VMEM((page, dk), F32)],
    )
    return pl.pallas_call(
        functools.partial(_sb_sample_kernel, g=g, tn=tn, nh=nh, hd=hd),
        grid_spec=grid_spec,
        out_shape=jax.ShapeDtypeStruct((nb, tn, dk), BF16),
        compiler_params=_params(2), name="sb_sample",
    )(page_table, q_rep, bias_rows, k_new, v_new, *([cache_k] * g), *([cache_v] * g))


def _merge_kernel(x_ref, cn_ref, att_ref, g0_ref, g1_ref, wco_ref, wao_ref, wo_ref, n2_ref, wq_ref,
                  x1_ref, hb_ref, qp_ref):
    merged = g0_ref[...] * _dot(cn_ref[...], wco_ref[...]) + g1_ref[...] * _dot(att_ref[...], wao_ref[...])
    x1 = x_ref[...] + _dot(merged.astype(BF16), wo_ref[...])
    x1_ref[...] = x1
    hb = _rms(x1, n2_ref[...]).astype(BF16)
    hb_ref[...] = hb
    qp = _dot(hb, wq_ref[...])
    for s in range(qp_ref.shape[0]):
        qp_ref[s] = qp[:, s * LANES:(s + 1) * LANES]


def _merge(x, cn, att, g0, g1, wco, wao, wo, n2, wq):
    t, dm = x.shape
    nq = wq.shape[1]
    assert nq % LANES == 0
    tb = min(256, t)
    assert t % tb == 0
    row = lambda a: pl.BlockSpec((tb, a.shape[1]), lambda i: (i, 0))
    full = lambda a: pl.BlockSpec(a.shape, lambda i: (0,) * a.ndim)
    n2r = n2.reshape(1, dm)
    return pl.pallas_call(
        _merge_kernel,
        grid=(t // tb,),
        in_specs=[row(x), row(cn), row(att), row(g0), row(g1), full(wco), full(wao), full(wo), full(n2r), full(wq)],
        out_specs=[row(x), row(x), pl.BlockSpec((nq // LANES, tb, LANES), lambda i: (0, i, 0))],
        out_shape=[jax.ShapeDtypeStruct((t, dm), F32), jax.ShapeDtypeStruct((t, dm), BF16),
                   jax.ShapeDtypeStruct((nq // LANES, t, LANES), F32)],
        compiler_params=_params(1), name="merge",
    )(x, cn, att, g0, g1, wco, wao, wo, n2r, wq)


def _oddeven_merge_sort_pairs(n):
    pairs = []
    p = 1
    while p < n:
        k = p
        while k >= 1:
            for j in range(k % p, n - k, 2 * k):
                for i in range(min(k, n - j - k)):
                    if (i + j) // (2 * p) == (i + j + k) // (2 * p):
                        pairs.append((i + j, i + j + k))
            k //= 2
        p *= 2
    return pairs


_SORT_PAIRS = _oddeven_merge_sort_pairs(TOPK)


def _exchange(rows, pairs):
    rows = list(rows)
    for a, b in pairs:
        hi, lo = jnp.maximum(rows[a], rows[b]), jnp.minimum(rows[a], rows[b])
        rows[a], rows[b] = hi, lo
    return rows


def _bitonic_merge(rows):
    d = TOPK // 2
    while d >= 1:
        rows = _exchange(rows, [(v, v + d) for v in range(TOPK) if not v & d])
        d //= 2
    return rows


def _top_of_union(a, b):
    return _bitonic_merge([jnp.maximum(a[v], b[TOPK - 1 - v]) for v in range(TOPK)])


def _merge_sublanes(rows):
    shift = SUBLANES // 2
    while shift >= 1:
        rows = _top_of_union(rows, [pltpu.roll(x, shift, axis=0) for x in rows])
        shift //= 2
    return rows


def _peer_select_kernel(qp_ref, keys_ref, s_ref, p_ref, tau_ref):
    nkeys = keys_ref.shape[1]
    groups = nkeys // SUBLANES
    tb = qp_ref.shape[1]
    sub = lax.broadcasted_iota(jnp.int32, (SUBLANES, tb), 0)

    def head(h, carry):
        tops, raws = [], []
        for side in range(2):
            hs = 2 * h + side
            st = lax.dot_general(keys_ref[hs], qp_ref[hs].astype(BF16), _NT, preferred_element_type=F32)
            s_ref[hs] = st
            raw = [st[SUBLANES * v:SUBLANES * (v + 1), :] for v in range(groups)]
            raws.append(raw)
            tops.append(_merge_sublanes(_exchange(raw, _SORT_PAIRS)))
        a, b = tops
        blo, bhi = b[0], b[SUBLANES]
        for r_ in range(1, SUBLANES):
            blo = jnp.where(sub == r_, b[r_], blo)
            bhi = jnp.where(sub == r_, b[SUBLANES + r_], bhi)
        lo = [a[i] + blo for i in range(TOPK)]
        hi = [a[i] + bhi for i in range(TOPK)]
        top = _merge_sublanes(_top_of_union(lo, hi))
        zsum = jnp.ones_like(top[0])
        for v in range(1, TOPK):
            zsum = zsum + jnp.exp(top[v] - top[0])
        inv = 1.0 / zsum
        tau_ref[h] = top[TOPK - 1]
        for v in range(groups):
            rs = slice(SUBLANES * v, SUBLANES * (v + 1))
            p_ref[2 * h, rs, :] = jnp.exp(raws[0][v] - a[0]) * inv
            p_ref[2 * h + 1, rs, :] = jnp.exp(raws[1][v] - b[0])
        return carry

    lax.fori_loop(0, tau_ref.shape[0], head, 0)


def _peer_select(qp3, keys_bf):
    hs2, t, dk = qp3.shape
    nkeys = keys_bf.shape[1]
    assert nkeys == TOPK * SUBLANES and dk == LANES
    tb = min(512, t)
    assert t % tb == 0
    tok = lambda lead, mid: pl.BlockSpec((lead, mid, tb), lambda i: (0, 0, i))
    return pl.pallas_call(
        _peer_select_kernel,
        grid=(t // tb,),
        in_specs=[pl.BlockSpec((hs2, tb, dk), lambda i: (0, i, 0)),
                  pl.BlockSpec(keys_bf.shape, lambda i: (0, 0, 0))],
        out_specs=[tok(hs2, nkeys), tok(hs2, nkeys), tok(hs2 // 2, SUBLANES)],
        out_shape=[jax.ShapeDtypeStruct((hs2, nkeys, t), F32), jax.ShapeDtypeStruct((hs2, nkeys, t), F32),
                   jax.ShapeDtypeStruct((hs2 // 2, SUBLANES, t), F32)],
        compiler_params=_params(1), name="peer_select",
    )(qp3, keys_bf)


def _gelu(x):
    return 0.5 * x * (1.0 + jnp.tanh(0.7978845608028654 * (x + 0.044715 * (x * x * x))))


def _peer_dense_kernel(hb_ref, x1_ref, s1_ref, p1_ref, s2_ref, p2_ref, tau_ref, u_ref, vt_ref, fg_ref,
                       y_ref, a_ref, w_ref, acc_ref, *, rb):
    c = pl.program_id(1)
    ec, tb = a_ref.shape
    nh, nkeys, _ = s2_ref.shape
    ni = ec // nkeys

    @pl.when(c == 0)
    def _():
        acc_ref[...] = jnp.zeros_like(acc_ref)

    a_ref[...] = lax.dot_general(u_ref[...], hb_ref[...], _NT, preferred_element_type=F32)

    def tile(il, carry):
        ig = c * ni + il
        r0 = pl.multiple_of(il * nkeys, nkeys)
        for lb in range(tb // LANES):
            ls = slice(lb * LANES, (lb + 1) * LANES)
            for rs in range(0, nkeys, rb):
                gate = jnp.zeros((rb, LANES), F32)
                for h in range(nh):
                    s = s1_ref[ig, h:h + 1, ls] + s2_ref[h, rs:rs + rb, ls]
                    pp = p1_ref[ig, h:h + 1, ls] * p2_ref[h, rs:rs + rb, ls]
                    gate = gate + jnp.where(s >= tau_ref[h, 0:1, ls], pp, 0.0)
                rows = pl.ds(r0 + rs, rb)
                w_ref[rows, ls] = (_gelu(a_ref[rows, ls]) * gate).astype(BF16)
        return carry

    lax.fori_loop(0, ni, tile, 0)
    acc_ref[...] += _dot(vt_ref[...], w_ref[...])

    @pl.when(c == pl.num_programs(1) - 1)
    def _():
        y_ref[...] = _rms(x1_ref[...] + acc_ref[...].T, fg_ref[...])


def _peer_dense(hb, x1, s1x, p1x, s2, p2, tau, u_bf, vt_bf, fg):
    t, dm = x1.shape
    ne = u_bf.shape[0]
    nh, nkeys, _ = s2.shape
    tb = min(512, t)
    ec = 8 * nkeys
    assert t % tb == 0 and ne % ec == 0 and ne == nkeys * nkeys
    tok3 = lambda a: pl.BlockSpec(a.shape[:2] + (tb,), lambda i, c: (0, 0, i))
    row = lambda a: pl.BlockSpec((tb, a.shape[1]), lambda i, c: (i, 0))
    fgr = fg.reshape(1, dm)
    return pl.pallas_call(
        functools.partial(_peer_dense_kernel, rb=64),
        grid=(t // tb, ne // ec),
        in_specs=[row(hb), row(x1), tok3(s1x), tok3(p1x), tok3(s2), tok3(p2), tok3(tau),
                  pl.BlockSpec((ec, dm), lambda i, c: (c, 0)),
                  pl.BlockSpec((dm, ec), lambda i, c: (0, c)),
                  pl.BlockSpec((1, dm), lambda i, c: (0, 0))],
        out_specs=row(x1),
        out_shape=jax.ShapeDtypeStruct((t, dm), F32),
        scratch_shapes=[pltpu.VMEM((ec, tb), F32), pltpu.VMEM((ec, tb), BF16), pltpu.VMEM((dm, tb), F32)],
        compiler_params=_params(2), name="peer_dense",
    )(hb, x1, s1x, p1x, s2, p2, tau, u_bf, vt_bf, fgr)


def _peer(x1, hb, qp3, keys_bf, u_bf, vt_bf, fg):
    s, p, tau = _peer_select(qp3, keys_bf)
    s1x = jnp.transpose(s[0::2], (1, 0, 2))
    p1x = jnp.transpose(p[0::2], (1, 0, 2))
    return _peer_dense(hb, x1, s1x, p1x, s[1::2], p[1::2], tau, u_bf, vt_bf, fg)


def kernel(x_prompt, x_sample, cache_k, cache_v, state_conv, page_table, norm1_g, w_in, dw_w, dw_b,
           conv_ln_g, conv_ln_b, w_conv_out, w_att_out, sb_bias, w_out, norm2_g, peer_wq, peer_keys,
           peer_u, peer_v, final_g):
    depth, npool, page, nh, hd = cache_k.shape
    bp, sp_len, dm = x_prompt.shape
    nb, tn, _ = x_sample.shape
    dc = w_conv_out.shape[1]
    da = nh * hd
    width = dw_w.shape[1]
    qscale = hd ** -0.5
    assert depth == 1, "the final rmsnorm is fused into the layer's last stage"
    assert hd & (hd - 1) == 0 and hd.bit_length() % 2 == 1, "the score scale is folded into q as a power of two"
    pages_per_step = 4

    xp = x_prompt.reshape(bp * sp_len, dm)
    xs = x_sample.reshape(nb * tn, dm)
    outs = {k: [] for k in ("kp", "vp", "cp", "ks", "vs", "cs")}
    for l in range(depth):
        w_in_bf = w_in[l].astype(BF16)
        wco, wao, wo, wq = (w_conv_out[l].astype(BF16), w_att_out[l].astype(BF16), w_out[l].astype(BF16),
                            peer_wq[l].astype(BF16))
        keys_bf = peer_keys[l].astype(BF16)
        keys_bf = jnp.transpose(keys_bf, (1, 0, 2, 3)).reshape((-1,) + keys_bf.shape[2:])
        u_bf = peer_u[l].astype(BF16)
        vt_bf = jnp.transpose(peer_v[l]).astype(BF16)
        conv_w = (dw_w[l], dw_b[l], conv_ln_g[l], conv_ln_b[l])

        u, k, v, qb, kb, vb, g0, g1 = _inproj(xp, norm1_g[l], w_in_bf, dc, da, qscale)
        cn, att = [], []
        for b in range(bp):
            rows = slice(b * sp_len, (b + 1) * sp_len)
            cn.append(_conv_prompt(u[rows], *conv_w))
            att.append(_sb_prompt(qb[rows], kb[rows], vb[rows], sb_bias[l], hd))
        cn = cn[0] if bp == 1 else jnp.concatenate(cn)
        att = att[0] if bp == 1 else jnp.concatenate(att)
        x1, hb, qp3 = _merge(xp, cn, att, g0, g1, wco, wao, wo, norm2_g[l], wq)
        xp = _peer(x1, hb, qp3, keys_bf, u_bf, vt_bf, final_g)
        u3 = u.reshape(bp, sp_len, dc)
        buf = jnp.concatenate([jnp.zeros((bp, width - 1, dc), u.dtype), u3], axis=1)[:, -(width - 1):]
        outs["kp"].append(k.reshape(bp, sp_len, nh, hd))
        outs["vp"].append(v.reshape(bp, sp_len, nh, hd))
        outs["cp"].append(buf)

        u, k, v, qb, kb, vb, g0, g1 = _inproj(xs, norm1_g[l], w_in_bf, dc, da, qscale)
        ext = jnp.concatenate([state_conv[l], u.reshape(nb, tn, dc)], axis=1)
        cn = _conv_sample(jnp.transpose(ext, (1, 0, 2)), *conv_w, tn)
        cn = jnp.transpose(cn, (1, 0, 2)).reshape(nb * tn, dc)
        q_rep = jnp.repeat(qb.reshape(nb, tn, da), nh, axis=1)
        bias_rows = jnp.broadcast_to(jnp.tile(sb_bias[l], tn)[:, None], (tn * nh, page)).astype(F32)
        pad = ((0, 0), (0, SUBLANES - tn % SUBLANES if tn % SUBLANES else 0), (0, 0))
        k_new = jnp.pad(k.reshape(nb, tn, da), pad)
        v_new = jnp.pad(v.reshape(nb, tn, da), pad)
        att = _sb_sample(q_rep, bias_rows, k_new, v_new, cache_k[l].reshape(npool, page, da),
                         cache_v[l].reshape(npool, page, da), page_table, tn, nh, hd, pages_per_step)
        x1, hb, qp3 = _merge(xs, cn, att.reshape(nb * tn, da), g0, g1, wco, wao, wo, norm2_g[l], wq)
        xs = _peer(x1, hb, qp3, keys_bf, u_bf, vt_bf, final_g)
        outs["ks"].append(k.reshape(nb, tn, nh, hd))
        outs["vs"].append(v.reshape(nb, tn, nh, hd))
        outs["cs"].append(ext[:, -(width - 1):])

    return (xp.reshape(bp, sp_len, dm), xs.reshape(nb, tn, dm), jnp.stack(outs["kp"]), jnp.stack(outs["vp"]),
            jnp.stack(outs["cp"]), jnp.stack(outs["ks"]), jnp.stack(outs["vs"]), jnp.stack(outs["cs"]))
```

```python
import functools

import jax
import jax.numpy as jnp
from jax import lax
from jax.experimental import pallas as pl
from jax.experimental.pallas import tpu as pltpu

EPS = 1e-6
TOPK = 16
F32 = jnp.float32
BF16 = jnp.bfloat16
LANES = 128
SUBLANES = 8
VMEM_LIMIT = 56 * 1024 * 1024

_NT = (((1,), (1,)), ((), ()))


def _params(n_axes):
    return pltpu.CompilerParams(dimension_semantics=("arbitrary",) * n_axes,
                                vmem_limit_bytes=VMEM_LIMIT)


def _sigmoid(x):
    return 1.0 / (1.0 + jnp.exp(-x))


def _softplus(z):
    return jnp.maximum(z, 0.0) + jnp.log(1.0 + jnp.exp(-jnp.abs(z)))


def _rms(x, g):
    ms = jnp.mean(x * x, axis=-1, keepdims=True)
    return (x * lax.rsqrt(ms + EPS)) * g


def _dot(a, b):
    return jnp.dot(a, b, preferred_element_type=F32)


def _suffix(x, tri):
    return _dot(x.astype(BF16), tri)


def _inproj_kernel(x_ref, g_ref, w_ref, u_ref, k_ref, v_ref, qb_ref, kb_ref, vb_ref, g0_ref, g1_ref,
                   *, dc, da, dm, qscale):
    h = _rms(x_ref[...], g_ref[...]).astype(BF16)

    def proj(c0, n):
        return _dot(h, w_ref[:, c0:c0 + n])

    u_ref[...] = proj(0, dc) * _sigmoid(proj(dc, dc))
    c0 = 2 * dc
    qb_ref[...] = (proj(c0, da) * qscale).astype(BF16)
    k = proj(c0 + da, da)
    k_ref[...] = k
    kb_ref[...] = k.astype(BF16)
    v = proj(c0 + 2 * da, da)
    v_ref[...] = v
    vb_ref[...] = v.astype(BF16)
    g0_ref[...] = _sigmoid(proj(c0 + 3 * da, dm))
    g1_ref[...] = _sigmoid(proj(c0 + 3 * da + dm, dm))


def _inproj(x, g, w_bf, dc, da, qscale):
    t, dm = x.shape
    tb = min(256, t)
    assert t % tb == 0
    row = lambda n: pl.BlockSpec((tb, n), lambda i: (i, 0))
    full = lambda a: pl.BlockSpec(a.shape, lambda i: (0,) * a.ndim)
    g2 = g.reshape(1, dm)
    return pl.pallas_call(
        functools.partial(_inproj_kernel, dc=dc, da=da, dm=dm, qscale=qscale),
        grid=(t // tb,),
        in_specs=[row(dm), full(g2), full(w_bf)],
        out_specs=[row(dc), row(da), row(da), row(da), row(da), row(da), row(dm), row(dm)],
        out_shape=[jax.ShapeDtypeStruct((t, dc), F32), jax.ShapeDtypeStruct((t, da), F32),
                   jax.ShapeDtypeStruct((t, da), F32), jax.ShapeDtypeStruct((t, da), BF16),
                   jax.ShapeDtypeStruct((t, da), BF16), jax.ShapeDtypeStruct((t, da), BF16),
                   jax.ShapeDtypeStruct((t, dm), F32), jax.ShapeDtypeStruct((t, dm), F32)],
        compiler_params=_params(1), name="inproj",
    )(x, g2, w_bf)


def _ln_silu(acc, lg, lb):
    mu = jnp.mean(acc, axis=-1, keepdims=True)
    d = acc - mu
    var = jnp.mean(d * d, axis=-1, keepdims=True)
    y = d * lax.rsqrt(var + EPS) * lg + lb
    return y * _sigmoid(y)


def _conv_prompt_kernel(u_ref, halo_ref, dw_ref, db_ref, lg_ref, lb_ref, o_ref, ext_ref,
                        *, tb, halo, width, chunk):
    first = pl.program_id(0) == 0
    ext_ref[0:halo, :] = jnp.where(first, 0.0, halo_ref[...])
    ext_ref[halo:halo + tb, :] = u_ref[...]
    off = halo - (width - 1)
    for r in range(0, tb, chunk):
        acc = jnp.zeros((chunk, u_ref.shape[1]), F32) + db_ref[...]
        for w in range(width):
            acc = acc + dw_ref[w:w + 1, :] * ext_ref[r + off + w:r + off + w + chunk, :]
        o_ref[r:r + chunk, :] = _ln_silu(acc, lg_ref[...], lb_ref[...]).astype(BF16)


def _conv_prompt(u, dw, db, lg, lb):
    t, dc = u.shape
    width = dw.shape[0]
    halo = -(-(width - 1) // SUBLANES) * SUBLANES
    tb = min(256, t)
    chunk = min(32, tb)
    assert t % tb == 0 and tb % halo == 0 and tb % chunk == 0
    per = tb // halo
    vec = lambda a: pl.BlockSpec((1, dc), lambda i: (0, 0))
    return pl.pallas_call(
        functools.partial(_conv_prompt_kernel, tb=tb, halo=halo, width=width, chunk=chunk),
        grid=(t // tb,),
        in_specs=[pl.BlockSpec((tb, dc), lambda i: (i, 0)),
                  pl.BlockSpec((halo, dc), lambda i: (jnp.maximum(i * per - 1, 0), 0)),
                  pl.BlockSpec((width, dc), lambda i: (0, 0)), vec(db), vec(lg), vec(lb)],
        out_specs=pl.BlockSpec((tb, dc), lambda i: (i, 0)),
        out_shape=jax.ShapeDtypeStruct((t, dc), BF16),
        scratch_shapes=[pltpu.VMEM((halo + tb, dc), F32)],
        compiler_params=_params(1), name="conv_prompt",
    )(u, u, dw, db.reshape(1, dc), lg.reshape(1, dc), lb.reshape(1, dc))


def _conv_sample_kernel(ext_ref, dw_ref, db_ref, lg_ref, lb_ref, o_ref, *, width, tn):
    for t in range(tn):
        acc = jnp.zeros(ext_ref.shape[1:], F32) + db_ref[...]
        for w in range(width):
            acc = acc + dw_ref[w:w + 1, :] * ext_ref[t + w]
        o_ref[t] = _ln_silu(acc, lg_ref[...], lb_ref[...]).astype(BF16)


def _conv_sample(ext_t, dw, db, lg, lb, tn):
    rows, nb, dc = ext_t.shape
    width = dw.shape[0]
    assert rows == width - 1 + tn
    vec = lambda a: pl.BlockSpec((1, dc), lambda i: (0, 0))
    return pl.pallas_call(
        functools.partial(_conv_sample_kernel, width=width, tn=tn),
        grid=(1,),
        in_specs=[pl.BlockSpec((rows, nb, dc), lambda i: (0, 0, 0)),
                  pl.BlockSpec((width, dc), lambda i: (0, 0)), vec(db), vec(lg), vec(lb)],
        out_specs=pl.BlockSpec((tn, nb, dc), lambda i: (0, 0, 0)),
        out_shape=jax.ShapeDtypeStruct((tn, nb, dc), BF16),
        compiler_params=_params(1), name="conv_sample",
    )(ext_t, dw, db.reshape(1, dc), lg.reshape(1, dc), lb.reshape(1, dc))


def _sb_prompt_kernel(bias_ref, q_ref, k_ref, v_ref, o_ref, acc_ref, *, tq, hd):
    p = pl.program_id(0)
    i = pl.program_id(1)
    lane = lax.broadcasted_iota(jnp.int32, (tq, LANES), 1)
    q = q_ref[...]
    zero = jnp.zeros_like(q)
    qh = (jnp.where(lane < hd, q, zero), jnp.where(lane >= hd, q, zero))
    bh = (bias_ref[2 * p], bias_ref[2 * p + 1])
    r = lax.broadcasted_iota(jnp.int32, (tq, tq), 0)
    c = lax.broadcasted_iota(jnp.int32, (tq, tq), 1)
    tri = (r >= c).astype(BF16)
    causal = c < r
    acc_ref[...] = jnp.zeros_like(acc_ref)

    def scores(j, hh, masked):
        kb = k_ref[pl.ds(pl.multiple_of(j * tq, tq), tq), :]
        z = lax.dot_general(qh[hh], kb, _NT, preferred_element_type=F32) + bh[hh]
        sp = _softplus(z)
        if masked:
            sp = jnp.where(causal, sp, 0.0)
        return z, _suffix(sp, tri)

    def finish(j, hh, z, cs, carry, masked):
        vb = v_ref[pl.ds(pl.multiple_of(j * tq, tq), tq), :]
        a = jnp.exp(z - cs - carry)
        if masked:
            a = jnp.where(causal, a, 0.0)
        acc_ref[hh] += _dot(a.astype(BF16), vb)
        return carry + cs[:, 0:1]

    def block(j, carries, masked):
        out = []
        for hh in range(2):
            z, cs = scores(j, hh, masked)
            out.append(finish(j, hh, z, cs, carries[hh], masked))
        return tuple(out)

    def two_blocks(j, carries):
        out = []
        for hh in range(2):
            z0, cs0 = scores(j, hh, False)
            z1, cs1 = scores(j - 1, hh, False)
            mid = finish(j, hh, z0, cs0, carries[hh], False)
            out.append(finish(j - 1, hh, z1, cs1, mid, False))
        return tuple(out)

    c0 = jnp.zeros((tq, 1), F32)
    carries = block(i, (c0, c0), True)
    carries = lax.fori_loop(0, i // 2, lambda it, cr: two_blocks(i - 1 - 2 * it, cr), carries)

    @pl.when(i % 2 == 1)
    def _():
        block(0, carries, False)

    o_ref[...] = jnp.where(lane < hd, acc_ref[0], acc_ref[1]).astype(BF16)


def _sb_prompt(qb, kb, vb, bias, hd):
    t, da = qb.shape
    assert 2 * hd == LANES and da % LANES == 0
    tq = min(256, t)
    assert t % tq == 0
    return pl.pallas_call(
        functools.partial(_sb_prompt_kernel, tq=tq, hd=hd),
        grid=(da // LANES, t // tq),
        in_specs=[pl.BlockSpec(memory_space=pltpu.SMEM),
                  pl.BlockSpec((tq, LANES), lambda p, i: (i, p)),
                  pl.BlockSpec((t, LANES), lambda p, i: (0, p)),
                  pl.BlockSpec((t, LANES), lambda p, i: (0, p))],
        out_specs=pl.BlockSpec((tq, LANES), lambda p, i: (i, p)),
        out_shape=jax.ShapeDtypeStruct((t, da), BF16),
        scratch_shapes=[pltpu.VMEM((2, tq, LANES), F32)],
        compiler_params=_params(2), name="sb_prompt",
    )(bias, qb, kb, vb)


def _sb_sample_kernel(pt_ref, q_ref, bias_ref, kn_ref, vn_ref, *rest, g, tn, nh):
    kp, vp = rest[:g], rest[g:2 * g]
    o_ref, acc_ref, carry_ref, kpad_ref, vpad_ref = rest[2 * g:]
    js = pl.program_id(1)
    rows = tn * nh
    page, _, hd = kpad_ref.shape
    width = page * nh
    ncol = width // LANES
    rr = lax.broadcasted_iota(jnp.int32, (rows, width), 0)
    ll = lax.broadcasted_iota(jnp.int32, (rows, width), 1)
    own = (ll % nh) == (rr % nh)
    r = lax.broadcasted_iota(jnp.int32, (LANES, LANES), 0)
    c = lax.broadcasted_iota(jnp.int32, (LANES, LANES), 1)
    tri = (r >= c).astype(BF16)
    q = q_ref[0]

    cols = [slice(ci * LANES, (ci + 1) * LANES) for ci in range(ncol)]

    def blocks(pages, keep):
        zs, css = [], []
        for k3, _ in pages:
            kf = k3.reshape(width, hd).astype(BF16)
            z = lax.dot_general(q, kf, _NT, preferred_element_type=F32) + bias_ref[...]
            sp = jnp.where(keep, _softplus(z), 0.0)
            zs.append(z)
            css.append(_suffix(jnp.concatenate([sp[:, cl] for cl in cols], axis=0), tri))
        run = carry_ref[...]
        newer = []
        for cs in css:
            offs = [None] * ncol
            for ci in reversed(range(ncol)):
                offs[ci] = run
                run = run + cs[ci * rows:(ci + 1) * rows, 0:1]
            newer.append(offs)
        carry_ref[...] = run
        for (_, v3), z, cs, offs in zip(pages, zs, css, newer):
            parts = [jnp.exp(z[:, cols[ci]] - cs[ci * rows:(ci + 1) * rows, :] - offs[ci]) for ci in range(ncol)]
            a = jnp.where(keep, jnp.concatenate(parts, axis=1), 0.0)
            acc_ref[...] += _dot(a.astype(BF16), v3.reshape(width, hd).astype(BF16))

    @pl.when(js == 0)
    def _():
        acc_ref[...] = jnp.zeros_like(acc_ref)
        carry_ref[...] = jnp.zeros_like(carry_ref)
        kpad_ref[...] = jnp.zeros_like(kpad_ref)
        vpad_ref[...] = jnp.zeros_like(vpad_ref)
        kpad_ref[0:tn] = kn_ref[0]
        vpad_ref[0:tn] = vn_ref[0]
        blocks([(kpad_ref[...], vpad_ref[...])], own & (ll // nh < rr // nh))

    @pl.when(js > 0)
    def _():
        blocks([(kp[gg][...], vp[gg][...]) for gg in range(g)], own)

    @pl.when(js == pl.num_programs(1) - 1)
    def _():
        o_ref[0] = acc_ref[...].astype(BF16)


def _sb_sample(q_rows, bias_rows, k_new, v_new, cache_k, cache_v, layer, page_table, g):
    nb, rows, hd = q_rows.shape
    _, _, page, nh, _ = cache_k.shape
    tn = k_new.shape[1]
    npages = page_table.shape[1]
    width = page * nh
    assert npages % g == 0 and rows == tn * nh and nh == SUBLANES and width % LANES == 0 and tn <= page

    def page_spec(gg):
        def imap(n, js, pt):
            lp = npages - 1 - ((jnp.maximum(js, 1) - 1) * g + gg)
            return (layer, pt[n, lp], 0, 0, 0)
        return pl.BlockSpec((None, None, page, nh, hd), imap)

    new_spec = pl.BlockSpec((1, tn, nh, hd), lambda n, js, pt: (n, 0, 0, 0))
    seq = pl.BlockSpec((1, rows, hd), lambda n, js, pt: (n, 0, 0))
    grid_spec = pltpu.PrefetchScalarGridSpec(
        num_scalar_prefetch=1,
        grid=(nb, 1 + npages // g),
        in_specs=[seq, pl.BlockSpec((rows, width), lambda n, js, pt: (0, 0)), new_spec, new_spec]
                 + [page_spec(gg) for gg in range(g)] * 2,
        out_specs=seq,
        scratch_shapes=[pltpu.VMEM((rows, hd), F32), pltpu.VMEM((rows, 1), F32),
                        pltpu.VMEM((page, nh, hd), F32), pltpu.VMEM((page, nh, hd), F32)],
    )
    return pl.pallas_call(
        functools.partial(_sb_sample_kernel, g=g, tn=tn, nh=nh),
        grid_spec=grid_spec,
        out_shape=jax.ShapeDtypeStruct((nb, rows, hd), BF16),
        compiler_params=_params(2), name="sb_sample",
    )(page_table, q_rows, bias_rows, k_new, v_new, *([cache_k] * g), *([cache_v] * g))


def _merge_kernel(x_ref, cn_ref, att_ref, g0_ref, g1_ref, wco_ref, wao_ref, wo_ref, n2_ref, wq_ref,
                  x1_ref, hb_ref, qp_ref):
    merged = g0_ref[...] * _dot(cn_ref[...], wco_ref[...]) + g1_ref[...] * _dot(att_ref[...], wao_ref[...])
    x1 = x_ref[...] + _dot(merged.astype(BF16), wo_ref[...])
    x1_ref[...] = x1
    hb = _rms(x1, n2_ref[...]).astype(BF16)
    hb_ref[...] = hb
    qp = _dot(hb, wq_ref[...])
    for s in range(qp_ref.shape[0]):
        qp_ref[s] = qp[:, s * LANES:(s + 1) * LANES]


def _merge(x, cn, att, g0, g1, wco, wao, wo, n2, wq):
    t, dm = x.shape
    nq = wq.shape[1]
    assert nq % LANES == 0
    tb = min(256, t)
    assert t % tb == 0
    row = lambda a: pl.BlockSpec((tb, a.shape[1]), lambda i: (i, 0))
    full = lambda a: pl.BlockSpec(a.shape, lambda i: (0,) * a.ndim)
    n2r = n2.reshape(1, dm)
    return pl.pallas_call(
        _merge_kernel,
        grid=(t // tb,),
        in_specs=[row(x), row(cn), row(att), row(g0), row(g1), full(wco), full(wao), full(wo), full(n2r), full(wq)],
        out_specs=[row(x), row(x), pl.BlockSpec((nq // LANES, tb, LANES), lambda i: (0, i, 0))],
        out_shape=[jax.ShapeDtypeStruct((t, dm), F32), jax.ShapeDtypeStruct((t, dm), BF16),
                   jax.ShapeDtypeStruct((nq // LANES, t, LANES), F32)],
        compiler_params=_params(1), name="merge",
    )(x, cn, att, g0, g1, wco, wao, wo, n2r, wq)


def _oddeven_merge_sort_pairs(n):
    pairs = []
    p = 1
    while p < n:
        k = p
        while k >= 1:
            for j in range(k % p, n - k, 2 * k):
                for i in range(min(k, n - j - k)):
                    if (i + j) // (2 * p) == (i + j + k) // (2 * p):
                        pairs.append((i + j, i + j + k))
            k //= 2
        p *= 2
    return pairs


_SORT_PAIRS = _oddeven_merge_sort_pairs(TOPK)


def _exchange(rows, pairs):
    rows = list(rows)
    for a, b in pairs:
        hi, lo = jnp.maximum(rows[a], rows[b]), jnp.minimum(rows[a], rows[b])
        rows[a], rows[b] = hi, lo
    return rows


def _bitonic_merge(rows):
    d = TOPK // 2
    while d >= 1:
        rows = _exchange(rows, [(v, v + d) for v in range(TOPK) if not v & d])
        d //= 2
    return rows


def _top_of_union(a, b):
    return _bitonic_merge([jnp.maximum(a[v], b[TOPK - 1 - v]) for v in range(TOPK)])


def _merge_sublanes(rows):
    shift = SUBLANES // 2
    while shift >= 1:
        rows = _top_of_union(rows, [pltpu.roll(x, shift, axis=0) for x in rows])
        shift //= 2
    return rows


def _peer_select_kernel(qp_ref, keys_ref, s_ref, p_ref, tau_ref):
    nkeys = keys_ref.shape[1]
    groups = nkeys // SUBLANES
    tb = qp_ref.shape[1]
    sub = lax.broadcasted_iota(jnp.int32, (SUBLANES, tb), 0)

    def head(h, carry):
        tops, raws = [], []
        for side in range(2):
            hs = 2 * h + side
            st = lax.dot_general(keys_ref[hs], qp_ref[hs].astype(BF16), _NT, preferred_element_type=F32)
            s_ref[side, h] = st
            raw = [st[SUBLANES * v:SUBLANES * (v + 1), :] for v in range(groups)]
            raws.append(raw)
            tops.append(_merge_sublanes(_exchange(raw, _SORT_PAIRS)))
        a, b = tops
        blo, bhi = b[0], b[SUBLANES]
        for r_ in range(1, SUBLANES):
            blo = jnp.where(sub == r_, b[r_], blo)
            bhi = jnp.where(sub == r_, b[SUBLANES + r_], bhi)
        lo = [a[i] + blo for i in range(TOPK)]
        hi = [a[i] + bhi for i in range(TOPK)]
        top = _merge_sublanes(_top_of_union(lo, hi))
        zsum = jnp.ones_like(top[0])
        for v in range(1, TOPK):
            zsum = zsum + jnp.exp(top[v] - top[0])
        inv = 1.0 / zsum
        tau_ref[h] = top[TOPK - 1]
        for v in range(groups):
            rs = slice(SUBLANES * v, SUBLANES * (v + 1))
            p_ref[0, h, rs, :] = jnp.exp(raws[0][v] - a[0]) * inv
            p_ref[1, h, rs, :] = jnp.exp(raws[1][v] - b[0])
        return carry

    lax.fori_loop(0, tau_ref.shape[0], head, 0)


def _peer_select(qp3, keys_bf):
    hs2, t, dk = qp3.shape
    nkeys = keys_bf.shape[1]
    assert nkeys == TOPK * SUBLANES and dk == LANES
    tb = min(512, t)
    assert t % tb == 0
    nh = hs2 // 2
    sides = pl.BlockSpec((2, nh, nkeys, tb), lambda i: (0, 0, 0, i))
    return pl.pallas_call(
        _peer_select_kernel,
        grid=(t // tb,),
        in_specs=[pl.BlockSpec((hs2, tb, dk), lambda i: (0, i, 0)),
                  pl.BlockSpec(keys_bf.shape, lambda i: (0, 0, 0))],
        out_specs=[sides, sides, pl.BlockSpec((nh, SUBLANES, tb), lambda i: (0, 0, i))],
        out_shape=[jax.ShapeDtypeStruct((2, nh, nkeys, t), F32), jax.ShapeDtypeStruct((2, nh, nkeys, t), F32),
                   jax.ShapeDtypeStruct((nh, SUBLANES, t), F32)],
        compiler_params=_params(1), name="peer_select",
    )(qp3, keys_bf)


def _gelu(x):
    return 0.5 * x * (1.0 + jnp.tanh(0.7978845608028654 * (x + 0.044715 * (x * x * x))))


def _peer_dense_kernel(hb_ref, x1_ref, s1_ref, p1_ref, s2_ref, p2_ref, tau_ref, u_ref, vt_ref, vtp_ref, fg_ref,
                       y_ref, w_ref, acc_ref, *, rb):
    c = pl.program_id(1)
    _, ec, tb = w_ref.shape
    nh, nkeys, _ = s2_ref.shape
    ni = ec // nkeys
    dr = acc_ref.shape[0] // ni
    cur = c % 2
    prev = 1 - cur

    @pl.when(c == 0)
    def _():
        acc_ref[...] = jnp.zeros_like(acc_ref)
        w_ref[prev] = jnp.zeros((ec, tb), BF16)

    def tile(il, carry):
        out_rows = pl.ds(pl.multiple_of(il * dr, dr), dr)
        acc_ref[out_rows, :] += _dot(vtp_ref[out_rows, :], w_ref[prev])
        ig = c * ni + il
        r0 = pl.multiple_of(il * nkeys, nkeys)
        a = lax.dot_general(u_ref[pl.ds(r0, nkeys), :], hb_ref[...], _NT, preferred_element_type=F32)
        for lb in range(tb // LANES):
            ls = slice(lb * LANES, (lb + 1) * LANES)
            for rs in range(0, nkeys, rb):
                gate = jnp.zeros((rb, LANES), F32)
                for h in range(nh):
                    s = s1_ref[ig, h:h + 1, ls] + s2_ref[h, rs:rs + rb, ls]
                    pp = p1_ref[ig, h:h + 1, ls] * p2_ref[h, rs:rs + rb, ls]
                    gate = gate + jnp.where(s >= tau_ref[h, 0:1, ls], pp, 0.0)
                w_ref[cur, pl.ds(r0 + rs, rb), ls] = (_gelu(a[rs:rs + rb, ls]) * gate).astype(BF16)
        return carry

    lax.fori_loop(0, ni, tile, 0)

    @pl.when(c == pl.num_programs(1) - 1)
    def _():
        acc = acc_ref[...] + _dot(vt_ref[...], w_ref[cur])
        y_ref[...] = _rms(x1_ref[...] + acc.T, fg_ref[...])


def _peer_dense(hb, x1, s1x, p1x, s2, p2, tau, u_bf, vt_bf, fg):
    t, dm = x1.shape
    ne = u_bf.shape[0]
    nh, nkeys, _ = s2.shape
    tb = min(512, t)
    ec = 8 * nkeys
    assert t % tb == 0 and ne % ec == 0 and ne == nkeys * nkeys and dm % (ec // nkeys) == 0
    tok3 = lambda a: pl.BlockSpec(a.shape[:2] + (tb,), lambda i, c: (0, 0, i))
    row = lambda a: pl.BlockSpec((tb, a.shape[1]), lambda i, c: (i, 0))
    fgr = fg.reshape(1, dm)
    return pl.pallas_call(
        functools.partial(_peer_dense_kernel, rb=64),
        grid=(t // tb, ne // ec),
        in_specs=[row(hb), row(x1), tok3(s1x), tok3(p1x), tok3(s2), tok3(p2), tok3(tau),
                  pl.BlockSpec((ec, dm), lambda i, c: (c, 0)),
                  pl.BlockSpec((dm, ec), lambda i, c: (0, ne // ec - 1)),
                  pl.BlockSpec((dm, ec), lambda i, c: (0, jnp.maximum(c - 1, 0))),
                  pl.BlockSpec((1, dm), lambda i, c: (0, 0))],
        out_specs=row(x1),
        out_shape=jax.ShapeDtypeStruct((t, dm), F32),
        scratch_shapes=[pltpu.VMEM((2, ec, tb), BF16), pltpu.VMEM((dm, tb), F32)],
        compiler_params=_params(2), name="peer_dense",
    )(hb, x1, s1x, p1x, s2, p2, tau, u_bf, vt_bf, vt_bf, fgr)


def _peer(x1, hb, qp3, keys_bf, u_bf, vt_bf, fg):
    s, p, tau = _peer_select(qp3, keys_bf)
    s1x = jnp.transpose(s[0], (1, 0, 2))
    p1x = jnp.transpose(p[0], (1, 0, 2))
    return _peer_dense(hb, x1, s1x, p1x, s[1], p[1], tau, u_bf, vt_bf, fg)


def kernel(x_prompt, x_sample, cache_k, cache_v, state_conv, page_table, norm1_g, w_in, dw_w, dw_b,
           conv_ln_g, conv_ln_b, w_conv_out, w_att_out, sb_bias, w_out, norm2_g, peer_wq, peer_keys,
           peer_u, peer_v, final_g):
    depth, _, page, nh, hd = cache_k.shape
    bp, sp_len, dm = x_prompt.shape
    nb, tn, _ = x_sample.shape
    dc = w_conv_out.shape[1]
    da = nh * hd
    width = dw_w.shape[1]
    qscale = hd ** -0.5
    assert depth == 1, "the final rmsnorm is fused into the layer's last stage"
    assert hd & (hd - 1) == 0 and hd.bit_length() % 2 == 1, "the score scale is folded into q as a power of two"
    pages_per_step = 4

    xp = x_prompt.reshape(bp * sp_len, dm)
    xs = x_sample.reshape(nb * tn, dm)
    outs = {k: [] for k in ("kp", "vp", "cp", "ks", "vs", "cs")}
    for l in range(depth):
        w_in_bf = w_in[l].astype(BF16)
        wco, wao, wo, wq = (w_conv_out[l].astype(BF16), w_att_out[l].astype(BF16), w_out[l].astype(BF16),
                            peer_wq[l].astype(BF16))
        keys_bf = peer_keys[l].astype(BF16)
        keys_bf = jnp.transpose(keys_bf, (1, 0, 2, 3)).reshape((-1,) + keys_bf.shape[2:])
        u_bf = peer_u[l].astype(BF16)
        vt_bf = jnp.transpose(peer_v[l]).astype(BF16)
        conv_w = (dw_w[l], dw_b[l], conv_ln_g[l], conv_ln_b[l])

        u, k, v, qb, kb, vb, g0, g1 = _inproj(xp, norm1_g[l], w_in_bf, dc, da, qscale)
        cn, att = [], []
        for b in range(bp):
            rows = slice(b * sp_len, (b + 1) * sp_len)
            cn.append(_conv_prompt(u[rows], *conv_w))
            att.append(_sb_prompt(qb[rows], kb[rows], vb[rows], sb_bias[l], hd))
        cn = cn[0] if bp == 1 else jnp.concatenate(cn)
        att = att[0] if bp == 1 else jnp.concatenate(att)
        x1, hb, qp3 = _merge(xp, cn, att, g0, g1, wco, wao, wo, norm2_g[l], wq)
        xp = _peer(x1, hb, qp3, keys_bf, u_bf, vt_bf, final_g)
        u3 = u.reshape(bp, sp_len, dc)
        buf = jnp.concatenate([jnp.zeros((bp, width - 1, dc), u.dtype), u3], axis=1)[:, -(width - 1):]
        outs["kp"].append(k.reshape(bp, sp_len, nh, hd))
        outs["vp"].append(v.reshape(bp, sp_len, nh, hd))
        outs["cp"].append(buf)

        u, k, v, qb, kb, vb, g0, g1 = _inproj(xs, norm1_g[l], w_in_bf, dc, da, qscale)
        ext = jnp.concatenate([state_conv[l], u.reshape(nb, tn, dc)], axis=1)
        cn = _conv_sample(jnp.transpose(ext, (1, 0, 2)), *conv_w, tn)
        cn = jnp.transpose(cn, (1, 0, 2)).reshape(nb * tn, dc)
        q_rows = qb.reshape(nb, tn * nh, hd)
        bias_rows = jnp.broadcast_to(jnp.tile(sb_bias[l], tn)[:, None], (tn * nh, page * nh)).astype(F32)
        att = _sb_sample(q_rows, bias_rows, k.reshape(nb, tn, nh, hd), v.reshape(nb, tn, nh, hd),
                         cache_k, cache_v, l, page_table, pages_per_step)
        x1, hb, qp3 = _merge(xs, cn, att.reshape(nb * tn, da), g0, g1, wco, wao, wo, norm2_g[l], wq)
        xs = _peer(x1, hb, qp3, keys_bf, u_bf, vt_bf, final_g)
        outs["ks"].append(k.reshape(nb, tn, nh, hd))
        outs["vs"].append(v.reshape(nb, tn, nh, hd))
        outs["cs"].append(ext[:, -(width - 1):])

    return (xp.reshape(bp, sp_len, dm), xs.reshape(nb, tn, dm), jnp.stack(outs["kp"]), jnp.stack(outs["vp"]),
            jnp.stack(outs["cp"]), jnp.stack(outs["ks"]), jnp.stack(outs["vs"]), jnp.stack(outs["cs"]))
```

```python
import functools

import jax
import jax.numpy as jnp
from jax import lax
from jax.experimental import pallas as pl
from jax.experimental.pallas import tpu as pltpu

EPS = 1e-6
TOPK = 16
F32 = jnp.float32
BF16 = jnp.bfloat16
LANES = 128
SUBLANES = 8
VMEM_LIMIT = 56 * 1024 * 1024

_NT = (((1,), (1,)), ((), ()))


def _params(n_axes):
    return pltpu.CompilerParams(dimension_semantics=("arbitrary",) * n_axes,
                                vmem_limit_bytes=VMEM_LIMIT)


def _sigmoid(x):
    return 1.0 / (1.0 + jnp.exp(-x))


def _softplus(z):
    return jnp.maximum(z, 0.0) + jnp.log(1.0 + jnp.exp(-jnp.abs(z)))


def _rms(x, g):
    ms = jnp.mean(x * x, axis=-1, keepdims=True)
    return (x * lax.rsqrt(ms + EPS)) * g


def _dot(a, b):
    return jnp.dot(a, b, preferred_element_type=F32)


def _suffix(x, tri):
    return _dot(x.astype(BF16), tri)


def _inproj_kernel(x_ref, g_ref, w_ref, u_ref, k_ref, v_ref, qb_ref, kb_ref, vb_ref, g0_ref, g1_ref,
                   *, dc, da, dm, qscale):
    h = _rms(x_ref[...], g_ref[...]).astype(BF16)

    def proj(c0, n):
        return _dot(h, w_ref[:, c0:c0 + n])

    u_ref[...] = proj(0, dc) * _sigmoid(proj(dc, dc))
    c0 = 2 * dc
    qb_ref[...] = (proj(c0, da) * qscale).astype(BF16)
    k = proj(c0 + da, da)
    k_ref[...] = k
    kb_ref[...] = k.astype(BF16)
    v = proj(c0 + 2 * da, da)
    v_ref[...] = v
    vb_ref[...] = v.astype(BF16)
    g0_ref[...] = _sigmoid(proj(c0 + 3 * da, dm))
    g1_ref[...] = _sigmoid(proj(c0 + 3 * da + dm, dm))


def _inproj(x, g, w_bf, dc, da, qscale):
    t, dm = x.shape
    tb = min(256, t)
    assert t % tb == 0
    row = lambda n: pl.BlockSpec((tb, n), lambda i: (i, 0))
    full = lambda a: pl.BlockSpec(a.shape, lambda i: (0,) * a.ndim)
    g2 = g.reshape(1, dm)
    return pl.pallas_call(
        functools.partial(_inproj_kernel, dc=dc, da=da, dm=dm, qscale=qscale),
        grid=(t // tb,),
        in_specs=[row(dm), full(g2), full(w_bf)],
        out_specs=[row(dc), row(da), row(da), row(da), row(da), row(da), row(dm), row(dm)],
        out_shape=[jax.ShapeDtypeStruct((t, dc), F32), jax.ShapeDtypeStruct((t, da), F32),
                   jax.ShapeDtypeStruct((t, da), F32), jax.ShapeDtypeStruct((t, da), BF16),
                   jax.ShapeDtypeStruct((t, da), BF16), jax.ShapeDtypeStruct((t, da), BF16),
                   jax.ShapeDtypeStruct((t, dm), F32), jax.ShapeDtypeStruct((t, dm), F32)],
        compiler_params=_params(1), name="inproj",
    )(x, g2, w_bf)


def _ln_silu(acc, lg, lb):
    mu = jnp.mean(acc, axis=-1, keepdims=True)
    d = acc - mu
    var = jnp.mean(d * d, axis=-1, keepdims=True)
    y = d * lax.rsqrt(var + EPS) * lg + lb
    return y * _sigmoid(y)


def _conv_prompt_kernel(u_ref, halo_ref, dw_ref, db_ref, lg_ref, lb_ref, o_ref, ext_ref,
                        *, tb, halo, width, chunk):
    first = pl.program_id(0) == 0
    ext_ref[0:halo, :] = jnp.where(first, 0.0, halo_ref[...])
    ext_ref[halo:halo + tb, :] = u_ref[...]
    off = halo - (width - 1)
    for r in range(0, tb, chunk):
        acc = jnp.zeros((chunk, u_ref.shape[1]), F32) + db_ref[...]
        for w in range(width):
            acc = acc + dw_ref[w:w + 1, :] * ext_ref[r + off + w:r + off + w + chunk, :]
        o_ref[r:r + chunk, :] = _ln_silu(acc, lg_ref[...], lb_ref[...]).astype(BF16)


def _conv_prompt(u, dw, db, lg, lb):
    t, dc = u.shape
    width = dw.shape[0]
    halo = -(-(width - 1) // SUBLANES) * SUBLANES
    tb = min(256, t)
    chunk = min(32, tb)
    assert t % tb == 0 and tb % halo == 0 and tb % chunk == 0
    per = tb // halo
    vec = lambda a: pl.BlockSpec((1, dc), lambda i: (0, 0))
    return pl.pallas_call(
        functools.partial(_conv_prompt_kernel, tb=tb, halo=halo, width=width, chunk=chunk),
        grid=(t // tb,),
        in_specs=[pl.BlockSpec((tb, dc), lambda i: (i, 0)),
                  pl.BlockSpec((halo, dc), lambda i: (jnp.maximum(i * per - 1, 0), 0)),
                  pl.BlockSpec((width, dc), lambda i: (0, 0)), vec(db), vec(lg), vec(lb)],
        out_specs=pl.BlockSpec((tb, dc), lambda i: (i, 0)),
        out_shape=jax.ShapeDtypeStruct((t, dc), BF16),
        scratch_shapes=[pltpu.VMEM((halo + tb, dc), F32)],
        compiler_params=_params(1), name="conv_prompt",
    )(u, u, dw, db.reshape(1, dc), lg.reshape(1, dc), lb.reshape(1, dc))


def _conv_sample_kernel(ext_ref, dw_ref, db_ref, lg_ref, lb_ref, o_ref, *, width, tn):
    for t in range(tn):
        acc = jnp.zeros(ext_ref.shape[1:], F32) + db_ref[...]
        for w in range(width):
            acc = acc + dw_ref[w:w + 1, :] * ext_ref[t + w]
        o_ref[t] = _ln_silu(acc, lg_ref[...], lb_ref[...]).astype(BF16)


def _conv_sample(ext_t, dw, db, lg, lb, tn):
    rows, nb, dc = ext_t.shape
    width = dw.shape[0]
    assert rows == width - 1 + tn
    vec = lambda a: pl.BlockSpec((1, dc), lambda i: (0, 0))
    return pl.pallas_call(
        functools.partial(_conv_sample_kernel, width=width, tn=tn),
        grid=(1,),
        in_specs=[pl.BlockSpec((rows, nb, dc), lambda i: (0, 0, 0)),
                  pl.BlockSpec((width, dc), lambda i: (0, 0)), vec(db), vec(lg), vec(lb)],
        out_specs=pl.BlockSpec((tn, nb, dc), lambda i: (0, 0, 0)),
        out_shape=jax.ShapeDtypeStruct((tn, nb, dc), BF16),
        compiler_params=_params(1), name="conv_sample",
    )(ext_t, dw, db.reshape(1, dc), lg.reshape(1, dc), lb.reshape(1, dc))


def _sb_prompt_kernel(bias_ref, q_ref, k_ref, v_ref, o_ref, acc_ref, *, tq, hd, nb):
    p = pl.program_id(0)
    i = pl.program_id(1)
    lane = lax.broadcasted_iota(jnp.int32, (tq, LANES), 1)
    q = q_ref[...]
    zero = jnp.zeros_like(q)
    qh = (jnp.where(lane < hd, q, zero), jnp.where(lane >= hd, q, zero))
    bh = (bias_ref[2 * p], bias_ref[2 * p + 1])
    r = lax.broadcasted_iota(jnp.int32, (tq, tq), 0)
    c = lax.broadcasted_iota(jnp.int32, (tq, tq), 1)
    tri = (r >= c).astype(BF16)
    causal = c < r
    acc_ref[...] = jnp.zeros_like(acc_ref)

    def scores(j, hh, masked):
        kb = k_ref[pl.ds(pl.multiple_of(j * tq, tq), tq), :]
        z = lax.dot_general(qh[hh], kb, _NT, preferred_element_type=F32) + bh[hh]
        sp = _softplus(z)
        if masked:
            sp = jnp.where(causal, sp, 0.0)
        return z, _suffix(sp, tri)

    def finish(j, hh, z, cs, carry, masked):
        vb = v_ref[pl.ds(pl.multiple_of(j * tq, tq), tq), :]
        a = jnp.exp(z - cs - carry)
        if masked:
            a = jnp.where(causal, a, 0.0)
        acc_ref[hh] += _dot(a.astype(BF16), vb)
        return carry + cs[:, 0:1]

    def block(j, carries, masked):
        out = []
        for hh in range(2):
            z, cs = scores(j, hh, masked)
            out.append(finish(j, hh, z, cs, carries[hh], masked))
        return tuple(out)

    def group(j, carries):
        out = []
        for hh in range(2):
            zc = [scores(j - n, hh, False) for n in range(nb)]
            carry = carries[hh]
            for n in range(nb):
                carry = finish(j - n, hh, zc[n][0], zc[n][1], carry, False)
            out.append(carry)
        return tuple(out)

    c0 = jnp.zeros((tq, 1), F32)
    carries = block(i, (c0, c0), True)
    carries = lax.fori_loop(0, i // nb, lambda it, cr: group(i - 1 - nb * it, cr), carries)
    rem = i % nb
    lax.fori_loop(0, rem, lambda it, cr: block(rem - 1 - it, cr, False), carries)
    o_ref[...] = jnp.where(lane < hd, acc_ref[0], acc_ref[1]).astype(BF16)


def _sb_prompt(qb, kb, vb, bias, hd):
    t, da = qb.shape
    assert 2 * hd == LANES and da % LANES == 0
    tq = min(256, t)
    assert t % tq == 0
    return pl.pallas_call(
        functools.partial(_sb_prompt_kernel, tq=tq, hd=hd, nb=4),
        grid=(da // LANES, t // tq),
        in_specs=[pl.BlockSpec(memory_space=pltpu.SMEM),
                  pl.BlockSpec((tq, LANES), lambda p, i: (i, p)),
                  pl.BlockSpec((t, LANES), lambda p, i: (0, p)),
                  pl.BlockSpec((t, LANES), lambda p, i: (0, p))],
        out_specs=pl.BlockSpec((tq, LANES), lambda p, i: (i, p)),
        out_shape=jax.ShapeDtypeStruct((t, da), BF16),
        scratch_shapes=[pltpu.VMEM((2, tq, LANES), F32)],
        compiler_params=_params(2), name="sb_prompt",
    )(bias, qb, kb, vb)


def _sb_sample_kernel(pt_ref, q_ref, bias_ref, kn_ref, vn_ref, *rest, g, tn, nh, hd):
    kp, vp = rest[:g], rest[g:2 * g]
    o_ref, acc_ref, carry_ref = rest[2 * g:]
    js = pl.program_id(1)
    rows, dk = acc_ref.shape
    page = kn_ref.shape[2]
    rr = lax.broadcasted_iota(jnp.int32, (rows, dk), 0)
    cc = lax.broadcasted_iota(jnp.int32, (rows, dk), 1)
    own = (cc // hd) == (rr % nh)
    qbd = jnp.where(own, q_ref[0], jnp.zeros((rows, dk), BF16))
    r = lax.broadcasted_iota(jnp.int32, (page, page), 0)
    c = lax.broadcasted_iota(jnp.int32, (page, page), 1)
    tri = (r >= c).astype(BF16)

    def blocks(pages, keymask):
        n = len(pages)
        span = n * page
        cols = [slice(gg * page, (gg + 1) * page) for gg in range(n)]
        kt = jnp.concatenate([k.astype(BF16) for k, _ in pages], axis=1)
        vt = jnp.concatenate([v.astype(BF16) for _, v in pages], axis=1)
        z = _dot(qbd, kt) + bias_ref[:, 0:span]
        sp = _softplus(z)
        if keymask is not None:
            sp = jnp.where(keymask, sp, 0.0)
        cs = _suffix(jnp.concatenate([sp[:, cl] for cl in cols], axis=0), tri)
        run = carry_ref[...]
        parts = []
        for gg in range(n):
            csg = cs[gg * rows:(gg + 1) * rows, :]
            parts.append(jnp.exp(z[:, cols[gg]] - csg - run))
            run = run + csg[:, 0:1]
        carry_ref[...] = run
        a = jnp.concatenate(parts, axis=1)
        if keymask is not None:
            a = jnp.where(keymask, a, 0.0)
        acc_ref[...] += lax.dot_general(a.astype(BF16), vt, _NT, preferred_element_type=F32)

    @pl.when(js == 0)
    def _():
        acc_ref[...] = jnp.zeros_like(acc_ref)
        carry_ref[...] = jnp.zeros_like(carry_ref)
        qt = lax.broadcasted_iota(jnp.int32, (rows, page), 0) // nh
        ks = lax.broadcasted_iota(jnp.int32, (rows, page), 1)
        blocks([(kn_ref[0], vn_ref[0])], ks < qt)

    @pl.when(js > 0)
    def _():
        blocks([(kp[gg][...].reshape(dk, page), vp[gg][...].reshape(dk, page)) for gg in range(g)], None)

    @pl.when(js == pl.num_programs(1) - 1)
    def _():
        m = jnp.where(own, acc_ref[...], 0.0)
        o_ref[0] = jnp.sum(m.reshape(tn, nh, dk), axis=1).astype(BF16)


def _sb_sample(q_rep, bias_rows, kt_new, vt_new, cache_kt, cache_vt, layer, page_table, tn, g):
    nb, rows, dk = q_rep.shape
    _, _, nh, hd, page = cache_kt.shape
    npages = page_table.shape[1]
    assert npages % g == 0 and rows == tn * nh and nh == SUBLANES and dk == nh * hd and tn <= page

    def page_spec(gg):
        def imap(n, js, pt):
            lp = npages - 1 - ((jnp.maximum(js, 1) - 1) * g + gg)
            return (layer, pt[n, lp], 0, 0, 0)
        return pl.BlockSpec((None, None, nh, hd, page), imap)

    seq = lambda shape: pl.BlockSpec((1,) + shape, lambda n, js, pt: (n, 0, 0))
    grid_spec = pltpu.PrefetchScalarGridSpec(
        num_scalar_prefetch=1,
        grid=(nb, 1 + npages // g),
        in_specs=[seq((rows, dk)), pl.BlockSpec((rows, g * page), lambda n, js, pt: (0, 0)),
                  seq((dk, page)), seq((dk, page))]
                 + [page_spec(gg) for gg in range(g)] * 2,
        out_specs=seq((tn, dk)),
        scratch_shapes=[pltpu.VMEM((rows, dk), F32), pltpu.VMEM((rows, 1), F32)],
    )
    return pl.pallas_call(
        functools.partial(_sb_sample_kernel, g=g, tn=tn, nh=nh, hd=hd),
        grid_spec=grid_spec,
        out_shape=jax.ShapeDtypeStruct((nb, tn, dk), BF16),
        compiler_params=_params(2), name="sb_sample",
    )(page_table, q_rep, bias_rows, kt_new, vt_new, *([cache_kt] * g), *([cache_vt] * g))


def _merge_kernel(x_ref, cn_ref, att_ref, g0_ref, g1_ref, wco_ref, wao_ref, wo_ref, n2_ref, wq_ref,
                  x1_ref, hb_ref, qp_ref):
    merged = g0_ref[...] * _dot(cn_ref[...], wco_ref[...]) + g1_ref[...] * _dot(att_ref[...], wao_ref[...])
    x1 = x_ref[...] + _dot(merged.astype(BF16), wo_ref[...])
    x1_ref[...] = x1
    hb = _rms(x1, n2_ref[...]).astype(BF16)
    hb_ref[...] = hb
    qp = _dot(hb, wq_ref[...])
    for s in range(qp_ref.shape[0]):
        qp_ref[s] = qp[:, s * LANES:(s + 1) * LANES]


def _merge(x, cn, att, g0, g1, wco, wao, wo, n2, wq):
    t, dm = x.shape
    nq = wq.shape[1]
    assert nq % LANES == 0
    tb = min(256, t)
    assert t % tb == 0
    row = lambda a: pl.BlockSpec((tb, a.shape[1]), lambda i: (i, 0))
    full = lambda a: pl.BlockSpec(a.shape, lambda i: (0,) * a.ndim)
    n2r = n2.reshape(1, dm)
    return pl.pallas_call(
        _merge_kernel,
        grid=(t // tb,),
        in_specs=[row(x), row(cn), row(att), row(g0), row(g1), full(wco), full(wao), full(wo), full(n2r), full(wq)],
        out_specs=[row(x), row(x), pl.BlockSpec((nq // LANES, tb, LANES), lambda i: (0, i, 0))],
        out_shape=[jax.ShapeDtypeStruct((t, dm), F32), jax.ShapeDtypeStruct((t, dm), BF16),
                   jax.ShapeDtypeStruct((nq // LANES, t, LANES), F32)],
        compiler_params=_params(1), name="merge",
    )(x, cn, att, g0, g1, wco, wao, wo, n2r, wq)


def _oddeven_merge_sort_pairs(n):
    pairs = []
    p = 1
    while p < n:
        k = p
        while k >= 1:
            for j in range(k % p, n - k, 2 * k):
                for i in range(min(k, n - j - k)):
                    if (i + j) // (2 * p) == (i + j + k) // (2 * p):
                        pairs.append((i + j, i + j + k))
            k //= 2
        p *= 2
    return pairs


_SORT_PAIRS = _oddeven_merge_sort_pairs(TOPK)


def _exchange(rows, pairs):
    rows = list(rows)
    for a, b in pairs:
        hi, lo = jnp.maximum(rows[a], rows[b]), jnp.minimum(rows[a], rows[b])
        rows[a], rows[b] = hi, lo
    return rows


def _bitonic_merge(rows):
    d = TOPK // 2
    while d >= 1:
        rows = _exchange(rows, [(v, v + d) for v in range(TOPK) if not v & d])
        d //= 2
    return rows


def _top_of_union(a, b):
    return _bitonic_merge([jnp.maximum(a[v], b[TOPK - 1 - v]) for v in range(TOPK)])


def _merge_sublanes(rows):
    shift = SUBLANES // 2
    while shift >= 1:
        rows = _top_of_union(rows, [pltpu.roll(x, shift, axis=0) for x in rows])
        shift //= 2
    return rows


def _peer_select_kernel(qp_ref, keys_ref, s_ref, p_ref, tau_ref):
    nkeys = keys_ref.shape[1]
    groups = nkeys // SUBLANES
    tb = qp_ref.shape[1]
    sub = lax.broadcasted_iota(jnp.int32, (SUBLANES, tb), 0)

    def head(h, carry):
        tops, raws = [], []
        for side in range(2):
            hs = 2 * h + side
            st = lax.dot_general(keys_ref[hs], qp_ref[hs].astype(BF16), _NT, preferred_element_type=F32)
            s_ref[side, h] = st
            raw = [st[SUBLANES * v:SUBLANES * (v + 1), :] for v in range(groups)]
            raws.append(raw)
            tops.append(_merge_sublanes(_exchange(raw, _SORT_PAIRS)))
        a, b = tops
        blo, bhi = b[0], b[SUBLANES]
        for r_ in range(1, SUBLANES):
            blo = jnp.where(sub == r_, b[r_], blo)
            bhi = jnp.where(sub == r_, b[SUBLANES + r_], bhi)
        lo = [a[i] + blo for i in range(TOPK)]
        hi = [a[i] + bhi for i in range(TOPK)]
        top = _merge_sublanes(_top_of_union(lo, hi))
        zsum = jnp.ones_like(top[0])
        for v in range(1, TOPK):
            zsum = zsum + jnp.exp(top[v] - top[0])
        inv = 1.0 / zsum
        tau_ref[h] = top[TOPK - 1]
        for v in range(groups):
            rs = slice(SUBLANES * v, SUBLANES * (v + 1))
            p_ref[0, h, rs, :] = jnp.exp(raws[0][v] - a[0]) * inv
            p_ref[1, h, rs, :] = jnp.exp(raws[1][v] - b[0])
        return carry

    lax.fori_loop(0, tau_ref.shape[0], head, 0)


def _peer_select(qp3, keys_bf):
    hs2, t, dk = qp3.shape
    nkeys = keys_bf.shape[1]
    assert nkeys == TOPK * SUBLANES and dk == LANES
    tb = min(512, t)
    assert t % tb == 0
    nh = hs2 // 2
    sides = pl.BlockSpec((2, nh, nkeys, tb), lambda i: (0, 0, 0, i))
    return pl.pallas_call(
        _peer_select_kernel,
        grid=(t // tb,),
        in_specs=[pl.BlockSpec((hs2, tb, dk), lambda i: (0, i, 0)),
                  pl.BlockSpec(keys_bf.shape, lambda i: (0, 0, 0))],
        out_specs=[sides, sides, pl.BlockSpec((nh, SUBLANES, tb), lambda i: (0, 0, i))],
        out_shape=[jax.ShapeDtypeStruct((2, nh, nkeys, t), F32), jax.ShapeDtypeStruct((2, nh, nkeys, t), F32),
                   jax.ShapeDtypeStruct((nh, SUBLANES, t), F32)],
        compiler_params=_params(1), name="peer_select",
    )(qp3, keys_bf)


def _gelu(x):
    return 0.5 * x * (1.0 + jnp.tanh(0.7978845608028654 * (x + 0.044715 * (x * x * x))))


def _peer_dense_kernel(hb_ref, x1_ref, s1_ref, p1_ref, s2_ref, p2_ref, tau_ref, u_ref, vt_ref, fg_ref,
                       y_ref, a_ref, w_ref, acc_ref, *, rb):
    c = pl.program_id(1)
    ec, tb = a_ref.shape
    nh, nkeys, _ = s2_ref.shape
    ni = ec // nkeys

    @pl.when(c == 0)
    def _():
        acc_ref[...] = jnp.zeros_like(acc_ref)

    a_ref[...] = lax.dot_general(u_ref[...], hb_ref[...], _NT, preferred_element_type=F32)

    def tile(il):
        ig = c * ni + il
        r0 = il * nkeys
        for lb in range(tb // LANES):
            ls = slice(lb * LANES, (lb + 1) * LANES)
            for rs in range(0, nkeys, rb):
                gate = jnp.zeros((rb, LANES), F32)
                for h in range(nh):
                    s = s1_ref[ig, h:h + 1, ls] + s2_ref[h, rs:rs + rb, ls]
                    pp = p1_ref[ig, h:h + 1, ls] * p2_ref[h, rs:rs + rb, ls]
                    gate = gate + jnp.where(s >= tau_ref[h, 0:1, ls], pp, 0.0)
                rows = pl.ds(r0 + rs, rb)
                w_ref[rows, ls] = (_gelu(a_ref[rows, ls]) * gate).astype(BF16)

    for il in range(ni):
        tile(il)
    acc_ref[...] += _dot(vt_ref[...], w_ref[...])

    @pl.when(c == pl.num_programs(1) - 1)
    def _():
        y_ref[...] = _rms(x1_ref[...] + acc_ref[...].T, fg_ref[...])


def _peer_dense(hb, x1, s1x, p1x, s2, p2, tau, u_bf, vt_bf, fg):
    t, dm = x1.shape
    ne = u_bf.shape[0]
    nh, nkeys, _ = s2.shape
    tb = min(512, t)
    ec = 8 * nkeys
    assert t % tb == 0 and ne % ec == 0 and ne == nkeys * nkeys
    tok3 = lambda a: pl.BlockSpec(a.shape[:2] + (tb,), lambda i, c: (0, 0, i))
    row = lambda a: pl.BlockSpec((tb, a.shape[1]), lambda i, c: (i, 0))
    fgr = fg.reshape(1, dm)
    return pl.pallas_call(
        functools.partial(_peer_dense_kernel, rb=64),
        grid=(t // tb, ne // ec),
        in_specs=[row(hb), row(x1), tok3(s1x), tok3(p1x), tok3(s2), tok3(p2), tok3(tau),
                  pl.BlockSpec((ec, dm), lambda i, c: (c, 0)),
                  pl.BlockSpec((dm, ec), lambda i, c: (0, c)),
                  pl.BlockSpec((1, dm), lambda i, c: (0, 0))],
        out_specs=row(x1),
        out_shape=jax.ShapeDtypeStruct((t, dm), F32),
        scratch_shapes=[pltpu.VMEM((ec, tb), F32), pltpu.VMEM((ec, tb), BF16), pltpu.VMEM((dm, tb), F32)],
        compiler_params=_params(2), name="peer_dense",
    )(hb, x1, s1x, p1x, s2, p2, tau, u_bf, vt_bf, fgr)


def _peer(x1, hb, qp3, keys_bf, u_bf, vt_bf, fg):
    s, p, tau = _peer_select(qp3, keys_bf)
    s1x = jnp.transpose(s[0], (1, 0, 2))
    p1x = jnp.transpose(p[0], (1, 0, 2))
    return _peer_dense(hb, x1, s1x, p1x, s[1], p[1], tau, u_bf, vt_bf, fg)


def kernel(x_prompt, x_sample, cache_k, cache_v, state_conv, page_table, norm1_g, w_in, dw_w, dw_b,
           conv_ln_g, conv_ln_b, w_conv_out, w_att_out, sb_bias, w_out, norm2_g, peer_wq, peer_keys,
           peer_u, peer_v, final_g):
    depth, _, page, nh, hd = cache_k.shape
    bp, sp_len, dm = x_prompt.shape
    nb, tn, _ = x_sample.shape
    dc = w_conv_out.shape[1]
    da = nh * hd
    width = dw_w.shape[1]
    qscale = hd ** -0.5
    assert depth == 1, "the final rmsnorm is fused into the layer's last stage"
    assert hd & (hd - 1) == 0 and hd.bit_length() % 2 == 1, "the score scale is folded into q as a power of two"
    pages_per_step = 8
    cache_kt = jnp.transpose(cache_k, (0, 1, 3, 4, 2))
    cache_vt = jnp.transpose(cache_v, (0, 1, 3, 4, 2))

    xp = x_prompt.reshape(bp * sp_len, dm)
    xs = x_sample.reshape(nb * tn, dm)
    outs = {k: [] for k in ("kp", "vp", "cp", "ks", "vs", "cs")}
    for l in range(depth):
        w_in_bf = w_in[l].astype(BF16)
        wco, wao, wo, wq = (w_conv_out[l].astype(BF16), w_att_out[l].astype(BF16), w_out[l].astype(BF16),
                            peer_wq[l].astype(BF16))
        keys_bf = peer_keys[l].astype(BF16)
        keys_bf = jnp.transpose(keys_bf, (1, 0, 2, 3)).reshape((-1,) + keys_bf.shape[2:])
        u_bf = peer_u[l].astype(BF16)
        vt_bf = jnp.transpose(peer_v[l]).astype(BF16)
        conv_w = (dw_w[l], dw_b[l], conv_ln_g[l], conv_ln_b[l])

        u, k, v, qb, kb, vb, g0, g1 = _inproj(xp, norm1_g[l], w_in_bf, dc, da, qscale)
        cn, att = [], []
        for b in range(bp):
            rows = slice(b * sp_len, (b + 1) * sp_len)
            cn.append(_conv_prompt(u[rows], *conv_w))
            att.append(_sb_prompt(qb[rows], kb[rows], vb[rows], sb_bias[l], hd))
        cn = cn[0] if bp == 1 else jnp.concatenate(cn)
        att = att[0] if bp == 1 else jnp.concatenate(att)
        x1, hb, qp3 = _merge(xp, cn, att, g0, g1, wco, wao, wo, norm2_g[l], wq)
        xp = _peer(x1, hb, qp3, keys_bf, u_bf, vt_bf, final_g)
        u3 = u.reshape(bp, sp_len, dc)
        buf = jnp.concatenate([jnp.zeros((bp, width - 1, dc), u.dtype), u3], axis=1)[:, -(width - 1):]
        outs["kp"].append(k.reshape(bp, sp_len, nh, hd))
        outs["vp"].append(v.reshape(bp, sp_len, nh, hd))
        outs["cp"].append(buf)

        u, k, v, qb, kb, vb, g0, g1 = _inproj(xs, norm1_g[l], w_in_bf, dc, da, qscale)
        ext = jnp.concatenate([state_conv[l], u.reshape(nb, tn, dc)], axis=1)
        cn = _conv_sample(jnp.transpose(ext, (1, 0, 2)), *conv_w, tn)
        cn = jnp.transpose(cn, (1, 0, 2)).reshape(nb * tn, dc)
        q_rep = jnp.repeat(qb.reshape(nb, tn, da), nh, axis=1)
        bias_rows = jnp.broadcast_to(jnp.tile(sb_bias[l], tn)[:, None],
                                     (tn * nh, pages_per_step * page)).astype(F32)
        new_t = lambda a: jnp.pad(jnp.transpose(a.reshape(nb, tn, da), (0, 2, 1)), ((0, 0), (0, 0), (0, page - tn)))
        att = _sb_sample(q_rep, bias_rows, new_t(k), new_t(v), cache_kt, cache_vt, l, page_table, tn,
                         pages_per_step)
        x1, hb, qp3 = _merge(xs, cn, att.reshape(nb * tn, da), g0, g1, wco, wao, wo, norm2_g[l], wq)
        xs = _peer(x1, hb, qp3, keys_bf, u_bf, vt_bf, final_g)
        outs["ks"].append(k.reshape(nb, tn, nh, hd))
        outs["vs"].append(v.reshape(nb, tn, nh, hd))
        outs["cs"].append(ext[:, -(width - 1):])

    return (xp.reshape(bp, sp_len, dm), xs.reshape(nb, tn, dm), jnp.stack(outs["kp"]), jnp.stack(outs["vp"]),
            jnp.stack(outs["cp"]), jnp.stack(outs["ks"]), jnp.stack(outs["vs"]), jnp.stack(outs["cs"]))
```

```python
import functools

import jax
import jax.numpy as jnp
from jax import lax
from jax.experimental import pallas as pl
from jax.experimental.pallas import tpu as pltpu

EPS = 1e-6
TOPK = 16
F32 = jnp.float32
BF16 = jnp.bfloat16
LANES = 128
SUBLANES = 8
VMEM_LIMIT = 56 * 1024 * 1024

_NT = (((1,), (1,)), ((), ()))


def _params(n_axes):
    return pltpu.CompilerParams(dimension_semantics=("arbitrary",) * n_axes,
                                vmem_limit_bytes=VMEM_LIMIT)


def _sigmoid(x):
    return 1.0 / (1.0 + jnp.exp(-x))


def _softplus(z):
    return jnp.maximum(z, 0.0) + jnp.log(1.0 + jnp.exp(-jnp.abs(z)))


def _rms(x, g):
    ms = jnp.mean(x * x, axis=-1, keepdims=True)
    return (x * lax.rsqrt(ms + EPS)) * g


def _dot(a, b):
    return jnp.dot(a, b, preferred_element_type=F32)


def _suffix(x, tri):
    return _dot(x.astype(BF16), tri)


def _inproj_kernel(x_ref, g_ref, w_ref, u_ref, k_ref, v_ref, qb_ref, kb_ref, vb_ref, g0_ref, g1_ref,
                   *, dc, da, dm, qscale):
    h = _rms(x_ref[...], g_ref[...]).astype(BF16)

    def proj(c0, n):
        return _dot(h, w_ref[:, c0:c0 + n])

    u_ref[...] = proj(0, dc) * _sigmoid(proj(dc, dc))
    c0 = 2 * dc
    qb_ref[...] = (proj(c0, da) * qscale).astype(BF16)
    k = proj(c0 + da, da)
    k_ref[...] = k
    kb_ref[...] = k.astype(BF16)
    v = proj(c0 + 2 * da, da)
    v_ref[...] = v
    vb_ref[...] = v.astype(BF16)
    g0_ref[...] = _sigmoid(proj(c0 + 3 * da, dm))
    g1_ref[...] = _sigmoid(proj(c0 + 3 * da + dm, dm))


def _inproj(x, g, w_bf, dc, da, qscale):
    t, dm = x.shape
    tb = min(256, t)
    assert t % tb == 0
    row = lambda n: pl.BlockSpec((tb, n), lambda i: (i, 0))
    full = lambda a: pl.BlockSpec(a.shape, lambda i: (0,) * a.ndim)
    g2 = g.reshape(1, dm)
    return pl.pallas_call(
        functools.partial(_inproj_kernel, dc=dc, da=da, dm=dm, qscale=qscale),
        grid=(t // tb,),
        in_specs=[row(dm), full(g2), full(w_bf)],
        out_specs=[row(dc), row(da), row(da), row(da), row(da), row(da), row(dm), row(dm)],
        out_shape=[jax.ShapeDtypeStruct((t, dc), F32), jax.ShapeDtypeStruct((t, da), F32),
                   jax.ShapeDtypeStruct((t, da), F32), jax.ShapeDtypeStruct((t, da), BF16),
                   jax.ShapeDtypeStruct((t, da), BF16), jax.ShapeDtypeStruct((t, da), BF16),
                   jax.ShapeDtypeStruct((t, dm), F32), jax.ShapeDtypeStruct((t, dm), F32)],
        compiler_params=_params(1), name="inproj",
    )(x, g2, w_bf)


def _ln_silu(acc, lg, lb):
    mu = jnp.mean(acc, axis=-1, keepdims=True)
    d = acc - mu
    var = jnp.mean(d * d, axis=-1, keepdims=True)
    y = d * lax.rsqrt(var + EPS) * lg + lb
    return y * _sigmoid(y)


def _conv_prompt_kernel(u_ref, halo_ref, dw_ref, db_ref, lg_ref, lb_ref, o_ref, ext_ref,
                        *, tb, halo, width, chunk):
    first = pl.program_id(0) == 0
    ext_ref[0:halo, :] = jnp.where(first, 0.0, halo_ref[...])
    ext_ref[halo:halo + tb, :] = u_ref[...]
    off = halo - (width - 1)
    for r in range(0, tb, chunk):
        acc = jnp.zeros((chunk, u_ref.shape[1]), F32) + db_ref[...]
        for w in range(width):
            acc = acc + dw_ref[w:w + 1, :] * ext_ref[r + off + w:r + off + w + chunk, :]
        o_ref[r:r + chunk, :] = _ln_silu(acc, lg_ref[...], lb_ref[...]).astype(BF16)


def _conv_prompt(u, dw, db, lg, lb):
    t, dc = u.shape
    width = dw.shape[0]
    halo = -(-(width - 1) // SUBLANES) * SUBLANES
    tb = min(256, t)
    chunk = min(32, tb)
    assert t % tb == 0 and tb % halo == 0 and tb % chunk == 0
    per = tb // halo
    vec = lambda a: pl.BlockSpec((1, dc), lambda i: (0, 0))
    return pl.pallas_call(
        functools.partial(_conv_prompt_kernel, tb=tb, halo=halo, width=width, chunk=chunk),
        grid=(t // tb,),
        in_specs=[pl.BlockSpec((tb, dc), lambda i: (i, 0)),
                  pl.BlockSpec((halo, dc), lambda i: (jnp.maximum(i * per - 1, 0), 0)),
                  pl.BlockSpec((width, dc), lambda i: (0, 0)), vec(db), vec(lg), vec(lb)],
        out_specs=pl.BlockSpec((tb, dc), lambda i: (i, 0)),
        out_shape=jax.ShapeDtypeStruct((t, dc), BF16),
        scratch_shapes=[pltpu.VMEM((halo + tb, dc), F32)],
        compiler_params=_params(1), name="conv_prompt",
    )(u, u, dw, db.reshape(1, dc), lg.reshape(1, dc), lb.reshape(1, dc))


def _conv_sample_kernel(ext_ref, dw_ref, db_ref, lg_ref, lb_ref, o_ref, *, width, tn):
    for t in range(tn):
        acc = jnp.zeros(ext_ref.shape[1:], F32) + db_ref[...]
        for w in range(width):
            acc = acc + dw_ref[w:w + 1, :] * ext_ref[t + w]
        o_ref[t] = _ln_silu(acc, lg_ref[...], lb_ref[...]).astype(BF16)


def _conv_sample(ext_t, dw, db, lg, lb, tn):
    rows, nb, dc = ext_t.shape
    width = dw.shape[0]
    assert rows == width - 1 + tn
    vec = lambda a: pl.BlockSpec((1, dc), lambda i: (0, 0))
    return pl.pallas_call(
        functools.partial(_conv_sample_kernel, width=width, tn=tn),
        grid=(1,),
        in_specs=[pl.BlockSpec((rows, nb, dc), lambda i: (0, 0, 0)),
                  pl.BlockSpec((width, dc), lambda i: (0, 0)), vec(db), vec(lg), vec(lb)],
        out_specs=pl.BlockSpec((tn, nb, dc), lambda i: (0, 0, 0)),
        out_shape=jax.ShapeDtypeStruct((tn, nb, dc), BF16),
        compiler_params=_params(1), name="conv_sample",
    )(ext_t, dw, db.reshape(1, dc), lg.reshape(1, dc), lb.reshape(1, dc))


def _sb_prompt_kernel(bias_ref, q_ref, k_ref, v_ref, o_ref, acc_ref, *, tq, hd, nb):
    p = pl.program_id(0)
    i = pl.program_id(1)
    lane = lax.broadcasted_iota(jnp.int32, (tq, LANES), 1)
    q = q_ref[...]
    zero = jnp.zeros_like(q)
    qh = (jnp.where(lane < hd, q, zero), jnp.where(lane >= hd, q, zero))
    bh = (bias_ref[2 * p], bias_ref[2 * p + 1])
    r = lax.broadcasted_iota(jnp.int32, (tq, tq), 0)
    c = lax.broadcasted_iota(jnp.int32, (tq, tq), 1)
    tri = (r >= c).astype(BF16)
    causal = c < r
    acc_ref[...] = jnp.zeros_like(acc_ref)

    def scores(j, hh, masked):
        kb = k_ref[pl.ds(pl.multiple_of(j * tq, tq), tq), :]
        z = lax.dot_general(qh[hh], kb, _NT, preferred_element_type=F32) + bh[hh]
        sp = _softplus(z)
        if masked:
            sp = jnp.where(causal, sp, 0.0)
        return z, _suffix(sp, tri)

    def finish(j, hh, z, cs, carry, masked):
        vb = v_ref[pl.ds(pl.multiple_of(j * tq, tq), tq), :]
        a = jnp.exp(z - cs - carry)
        if masked:
            a = jnp.where(causal, a, 0.0)
        acc_ref[hh] += _dot(a.astype(BF16), vb)
        return carry + cs[:, 0:1]

    def block(j, carries, masked):
        out = []
        for hh in range(2):
            z, cs = scores(j, hh, masked)
            out.append(finish(j, hh, z, cs, carries[hh], masked))
        return tuple(out)

    def group(j, carries):
        out = []
        for hh in range(2):
            zc = [scores(j - n, hh, False) for n in range(nb)]
            carry = carries[hh]
            for n in range(nb):
                carry = finish(j - n, hh, zc[n][0], zc[n][1], carry, False)
            out.append(carry)
        return tuple(out)

    c0 = jnp.zeros((tq, 1), F32)
    carries = block(i, (c0, c0), True)
    carries = lax.fori_loop(0, i // nb, lambda it, cr: group(i - 1 - nb * it, cr), carries)
    rem = i % nb
    lax.fori_loop(0, rem, lambda it, cr: block(rem - 1 - it, cr, False), carries)
    o_ref[...] = jnp.where(lane < hd, acc_ref[0], acc_ref[1]).astype(BF16)


def _sb_prompt(qb, kb, vb, bias, hd):
    t, da = qb.shape
    assert 2 * hd == LANES and da % LANES == 0
    tq = min(256, t)
    assert t % tq == 0
    return pl.pallas_call(
        functools.partial(_sb_prompt_kernel, tq=tq, hd=hd, nb=4),
        grid=(da // LANES, t // tq),
        in_specs=[pl.BlockSpec(memory_space=pltpu.SMEM),
                  pl.BlockSpec((tq, LANES), lambda p, i: (i, p)),
                  pl.BlockSpec((t, LANES), lambda p, i: (0, p)),
                  pl.BlockSpec((t, LANES), lambda p, i: (0, p))],
        out_specs=pl.BlockSpec((tq, LANES), lambda p, i: (i, p)),
        out_shape=jax.ShapeDtypeStruct((t, da), BF16),
        scratch_shapes=[pltpu.VMEM((2, tq, LANES), F32)],
        compiler_params=_params(2), name="sb_prompt",
    )(bias, qb, kb, vb)


def _sb_sample_kernel(pt_ref, q_ref, bias_ref, kn_ref, vn_ref, *rest, g, tn, nh, hd):
    kp, vp = rest[:g], rest[g:2 * g]
    o_ref, acc_ref, carry_ref = rest[2 * g:]
    js = pl.program_id(1)
    rows, dk = acc_ref.shape
    page = kn_ref.shape[2]
    rr = lax.broadcasted_iota(jnp.int32, (rows, dk), 0)
    cc = lax.broadcasted_iota(jnp.int32, (rows, dk), 1)
    own = (cc // hd) == (rr % nh)
    qbd = jnp.where(own, q_ref[0], jnp.zeros((rows, dk), BF16))
    r = lax.broadcasted_iota(jnp.int32, (page, page), 0)
    c = lax.broadcasted_iota(jnp.int32, (page, page), 1)
    tri = (r >= c).astype(BF16)

    def blocks(pages, keymask):
        n = len(pages)
        span = n * page
        cols = [slice(gg * page, (gg + 1) * page) for gg in range(n)]
        kt = jnp.concatenate([k.astype(BF16) for k, _ in pages], axis=1)
        vt = jnp.concatenate([v.astype(BF16) for _, v in pages], axis=1)
        z = _dot(qbd, kt) + bias_ref[:, 0:span]
        sp = _softplus(z)
        if keymask is not None:
            sp = jnp.where(keymask, sp, 0.0)
        cs = _suffix(jnp.concatenate([sp[:, cl] for cl in cols], axis=0), tri)
        run = carry_ref[...]
        parts = []
        for gg in range(n):
            csg = cs[gg * rows:(gg + 1) * rows, :]
            parts.append(jnp.exp(z[:, cols[gg]] - csg - run))
            run = run + csg[:, 0:1]
        carry_ref[...] = run
        a = jnp.concatenate(parts, axis=1)
        if keymask is not None:
            a = jnp.where(keymask, a, 0.0)
        acc_ref[...] += lax.dot_general(a.astype(BF16), vt, _NT, preferred_element_type=F32)

    @pl.when(js == 0)
    def _():
        acc_ref[...] = jnp.zeros_like(acc_ref)
        carry_ref[...] = jnp.zeros_like(carry_ref)
        qt = lax.broadcasted_iota(jnp.int32, (rows, page), 0) // nh
        ks = lax.broadcasted_iota(jnp.int32, (rows, page), 1)
        blocks([(kn_ref[0], vn_ref[0])], ks < qt)

    @pl.when(js > 0)
    def _():
        blocks([(kp[gg][...].reshape(dk, page), vp[gg][...].reshape(dk, page)) for gg in range(g)], None)

    @pl.when(js == pl.num_programs(1) - 1)
    def _():
        m = jnp.where(own, acc_ref[...], 0.0)
        o_ref[0] = jnp.sum(m.reshape(tn, nh, dk), axis=1).astype(BF16)


def _sb_sample(q_rep, bias_rows, kt_new, vt_new, cache_kt, cache_vt, layer, page_table, tn, g):
    nb, rows, dk = q_rep.shape
    _, _, nh, hd, page = cache_kt.shape
    npages = page_table.shape[1]
    assert npages % g == 0 and rows == tn * nh and nh == SUBLANES and dk == nh * hd and tn <= page

    def page_spec(gg):
        def imap(n, js, pt):
            lp = npages - 1 - ((jnp.maximum(js, 1) - 1) * g + gg)
            return (layer, pt[n, lp], 0, 0, 0)
        return pl.BlockSpec((None, None, nh, hd, page), imap)

    seq = lambda shape: pl.BlockSpec((1,) + shape, lambda n, js, pt: (n, 0, 0))
    grid_spec = pltpu.PrefetchScalarGridSpec(
        num_scalar_prefetch=1,
        grid=(nb, 1 + npages // g),
        in_specs=[seq((rows, dk)), pl.BlockSpec((rows, g * page), lambda n, js, pt: (0, 0)),
                  seq((dk, page)), seq((dk, page))]
                 + [page_spec(gg) for gg in range(g)] * 2,
        out_specs=seq((tn, dk)),
        scratch_shapes=[pltpu.VMEM((rows, dk), F32), pltpu.VMEM((rows, 1), F32)],
    )
    return pl.pallas_call(
        functools.partial(_sb_sample_kernel, g=g, tn=tn, nh=nh, hd=hd),
        grid_spec=grid_spec,
        out_shape=jax.ShapeDtypeStruct((nb, tn, dk), BF16),
        compiler_params=_params(2), name="sb_sample",
    )(page_table, q_rep, bias_rows, kt_new, vt_new, *([cache_kt] * g), *([cache_vt] * g))


def _merge_kernel(x_ref, cn_ref, att_ref, g0_ref, g1_ref, wco_ref, wao_ref, wo_ref, n2_ref, wq_ref,
                  x1_ref, hb_ref, qp_ref):
    merged = g0_ref[...] * _dot(cn_ref[...], wco_ref[...]) + g1_ref[...] * _dot(att_ref[...], wao_ref[...])
    x1 = x_ref[...] + _dot(merged.astype(BF16), wo_ref[...])
    x1_ref[...] = x1
    hb = _rms(x1, n2_ref[...]).astype(BF16)
    hb_ref[...] = hb
    qp = _dot(hb, wq_ref[...])
    for s in range(qp_ref.shape[0]):
        qp_ref[s] = qp[:, s * LANES:(s + 1) * LANES]


def _merge(x, cn, att, g0, g1, wco, wao, wo, n2, wq):
    t, dm = x.shape
    nq = wq.shape[1]
    assert nq % LANES == 0
    tb = min(256, t)
    assert t % tb == 0
    row = lambda a: pl.BlockSpec((tb, a.shape[1]), lambda i: (i, 0))
    full = lambda a: pl.BlockSpec(a.shape, lambda i: (0,) * a.ndim)
    n2r = n2.reshape(1, dm)
    return pl.pallas_call(
        _merge_kernel,
        grid=(t // tb,),
        in_specs=[row(x), row(cn), row(att), row(g0), row(g1), full(wco), full(wao), full(wo), full(n2r), full(wq)],
        out_specs=[row(x), row(x), pl.BlockSpec((nq // LANES, tb, LANES), lambda i: (0, i, 0))],
        out_shape=[jax.ShapeDtypeStruct((t, dm), F32), jax.ShapeDtypeStruct((t, dm), BF16),
                   jax.ShapeDtypeStruct((nq // LANES, t, LANES), F32)],
        compiler_params=_params(1), name="merge",
    )(x, cn, att, g0, g1, wco, wao, wo, n2r, wq)


def _oddeven_merge_sort_pairs(n):
    pairs = []
    p = 1
    while p < n:
        k = p
        while k >= 1:
            for j in range(k % p, n - k, 2 * k):
                for i in range(min(k, n - j - k)):
                    if (i + j) // (2 * p) == (i + j + k) // (2 * p):
                        pairs.append((i + j, i + j + k))
            k //= 2
        p *= 2
    return pairs


_SORT_PAIRS = _oddeven_merge_sort_pairs(TOPK)


def _exchange(rows, pairs):
    rows = list(rows)
    for a, b in pairs:
        hi, lo = jnp.maximum(rows[a], rows[b]), jnp.minimum(rows[a], rows[b])
        rows[a], rows[b] = hi, lo
    return rows


def _bitonic_merge(rows):
    d = TOPK // 2
    while d >= 1:
        rows = _exchange(rows, [(v, v + d) for v in range(TOPK) if not v & d])
        d //= 2
    return rows


def _top_of_union(a, b):
    return _bitonic_merge([jnp.maximum(a[v], b[TOPK - 1 - v]) for v in range(TOPK)])


def _merge_sublanes(rows):
    shift = SUBLANES // 2
    while shift >= 1:
        rows = _top_of_union(rows, [pltpu.roll(x, shift, axis=0) for x in rows])
        shift //= 2
    return rows


def _peer_select_kernel(qp_ref, keys_ref, s_ref, p_ref, tau_ref):
    nkeys = keys_ref.shape[1]
    groups = nkeys // SUBLANES
    tb = qp_ref.shape[1]
    sub = lax.broadcasted_iota(jnp.int32, (SUBLANES, tb), 0)

    def head(h, carry):
        tops, raws = [], []
        for side in range(2):
            hs = 2 * h + side
            st = lax.dot_general(keys_ref[hs], qp_ref[hs].astype(BF16), _NT, preferred_element_type=F32)
            s_ref[side, h] = st
            raw = [st[SUBLANES * v:SUBLANES * (v + 1), :] for v in range(groups)]
            raws.append(raw)
            tops.append(_merge_sublanes(_exchange(raw, _SORT_PAIRS)))
        a, b = tops
        blo, bhi = b[0], b[SUBLANES]
        for r_ in range(1, SUBLANES):
            blo = jnp.where(sub == r_, b[r_], blo)
            bhi = jnp.where(sub == r_, b[SUBLANES + r_], bhi)
        lo = [a[i] + blo for i in range(TOPK)]
        hi = [a[i] + bhi for i in range(TOPK)]
        top = _merge_sublanes(_top_of_union(lo, hi))
        zsum = jnp.ones_like(top[0])
        for v in range(1, TOPK):
            zsum = zsum + jnp.exp(top[v] - top[0])
        inv = 1.0 / zsum
        tau_ref[h] = top[TOPK - 1]
        for v in range(groups):
            rs = slice(SUBLANES * v, SUBLANES * (v + 1))
            p_ref[0, h, rs, :] = jnp.exp(raws[0][v] - a[0]) * inv
            p_ref[1, h, rs, :] = jnp.exp(raws[1][v] - b[0])
        return carry

    lax.fori_loop(0, tau_ref.shape[0], head, 0)


def _peer_select(qp3, keys_bf):
    hs2, t, dk = qp3.shape
    nkeys = keys_bf.shape[1]
    assert nkeys == TOPK * SUBLANES and dk == LANES
    tb = min(512, t)
    assert t % tb == 0
    nh = hs2 // 2
    sides = pl.BlockSpec((2, nh, nkeys, tb), lambda i: (0, 0, 0, i))
    return pl.pallas_call(
        _peer_select_kernel,
        grid=(t // tb,),
        in_specs=[pl.BlockSpec((hs2, tb, dk), lambda i: (0, i, 0)),
                  pl.BlockSpec(keys_bf.shape, lambda i: (0, 0, 0))],
        out_specs=[sides, sides, pl.BlockSpec((nh, SUBLANES, tb), lambda i: (0, 0, i))],
        out_shape=[jax.ShapeDtypeStruct((2, nh, nkeys, t), F32), jax.ShapeDtypeStruct((2, nh, nkeys, t), F32),
                   jax.ShapeDtypeStruct((nh, SUBLANES, t), F32)],
        compiler_params=_params(1), name="peer_select",
    )(qp3, keys_bf)


def _gelu(x):
    return 0.5 * x * (1.0 + jnp.tanh(0.7978845608028654 * (x + 0.044715 * (x * x * x))))


def _peer_dense_kernel(hb_ref, x1_ref, s1_ref, p1_ref, s2_ref, p2_ref, tau_ref, u_ref, vt_ref, fg_ref,
                       y_ref, a_ref, w_ref, acc_ref, *, rb):
    c = pl.program_id(1)
    ec, tb = a_ref.shape
    nh, nkeys, _ = s2_ref.shape
    ni = ec // nkeys

    @pl.when(c == 0)
    def _():
        acc_ref[...] = jnp.zeros_like(acc_ref)

    a_ref[...] = lax.dot_general(u_ref[...], hb_ref[...], _NT, preferred_element_type=F32)

    def tile(il):
        ig = c * ni + il
        r0 = il * nkeys
        for lb in range(tb // LANES):
            ls = slice(lb * LANES, (lb + 1) * LANES)
            for rs in range(0, nkeys, rb):
                gate = jnp.zeros((rb, LANES), F32)
                for h in range(nh):
                    s = s1_ref[ig, h:h + 1, ls] + s2_ref[h, rs:rs + rb, ls]
                    pp = p1_ref[ig, h:h + 1, ls] * p2_ref[h, rs:rs + rb, ls]
                    gate = gate + jnp.where(s >= tau_ref[h, 0:1, ls], pp, 0.0)
                rows = pl.ds(r0 + rs, rb)
                w_ref[rows, ls] = (_gelu(a_ref[rows, ls]) * gate).astype(BF16)

    for il in range(ni):
        tile(il)
    acc_ref[...] += _dot(vt_ref[...], w_ref[...])

    @pl.when(c == pl.num_programs(1) - 1)
    def _():
        y_ref[...] = _rms(x1_ref[...] + acc_ref[...].T, fg_ref[...])


def _peer_dense(hb, x1, s1x, p1x, s2, p2, tau, u_bf, vt_bf, fg):
    t, dm = x1.shape
    ne = u_bf.shape[0]
    nh, nkeys, _ = s2.shape
    tb = min(512, t)
    ec = 8 * nkeys
    assert t % tb == 0 and ne % ec == 0 and ne == nkeys * nkeys
    tok3 = lambda a: pl.BlockSpec(a.shape[:2] + (tb,), lambda i, c: (0, 0, i))
    row = lambda a: pl.BlockSpec((tb, a.shape[1]), lambda i, c: (i, 0))
    fgr = fg.reshape(1, dm)
    return pl.pallas_call(
        functools.partial(_peer_dense_kernel, rb=64),
        grid=(t // tb, ne // ec),
        in_specs=[row(hb), row(x1), tok3(s1x), tok3(p1x), tok3(s2), tok3(p2), tok3(tau),
                  pl.BlockSpec((ec, dm), lambda i, c: (c, 0)),
                  pl.BlockSpec((dm, ec), lambda i, c: (0, c)),
                  pl.BlockSpec((1, dm), lambda i, c: (0, 0))],
        out_specs=row(x1),
        out_shape=jax.ShapeDtypeStruct((t, dm), F32),
        scratch_shapes=[pltpu.VMEM((ec, tb), F32), pltpu.VMEM((ec, tb), BF16), pltpu.VMEM((dm, tb), F32)],
        compiler_params=_params(2), name="peer_dense",
    )(hb, x1, s1x, p1x, s2, p2, tau, u_bf, vt_bf, fgr)


def _peer(x1, hb, qp3, keys_bf, u_bf, vt_bf, fg):
    s, p, tau = _peer_select(qp3, keys_bf)
    s1x = jnp.transpose(s[0], (1, 0, 2))
    p1x = jnp.transpose(p[0], (1, 0, 2))
    return _peer_dense(hb, x1, s1x, p1x, s[1], p[1], tau, u_bf, vt_bf, fg)


def kernel(x_prompt, x_sample, cache_k, cache_v, state_conv, page_table, norm1_g, w_in, dw_w, dw_b,
           conv_ln_g, conv_ln_b, w_conv_out, w_att_out, sb_bias, w_out, norm2_g, peer_wq, peer_keys,
           peer_u, peer_v, final_g):
    depth, _, page, nh, hd = cache_k.shape
    bp, sp_len, dm = x_prompt.shape
    nb, tn, _ = x_sample.shape
    dc = w_conv_out.shape[1]
    da = nh * hd
    width = dw_w.shape[1]
    qscale = hd ** -0.5
    assert depth == 1, "the final rmsnorm is fused into the layer's last stage"
    assert hd & (hd - 1) == 0 and hd.bit_length() % 2 == 1, "the score scale is folded into q as a power of two"
    pages_per_step = 16
    cache_kt = jnp.transpose(cache_k, (0, 1, 3, 4, 2))
    cache_vt = jnp.transpose(cache_v, (0, 1, 3, 4, 2))

    xp = x_prompt.reshape(bp * sp_len, dm)
    xs = x_sample.reshape(nb * tn, dm)
    outs = {k: [] for k in ("kp", "vp", "cp", "ks", "vs", "cs")}
    for l in range(depth):
        w_in_bf = w_in[l].astype(BF16)
        wco, wao, wo, wq = (w_conv_out[l].astype(BF16), w_att_out[l].astype(BF16), w_out[l].astype(BF16),
                            peer_wq[l].astype(BF16))
        keys_bf = peer_keys[l].astype(BF16)
        keys_bf = jnp.transpose(keys_bf, (1, 0, 2, 3)).reshape((-1,) + keys_bf.shape[2:])
        u_bf = peer_u[l].astype(BF16)
        vt_bf = jnp.transpose(peer_v[l]).astype(BF16)
        conv_w = (dw_w[l], dw_b[l], conv_ln_g[l], conv_ln_b[l])

        u, k, v, qb, kb, vb, g0, g1 = _inproj(xp, norm1_g[l], w_in_bf, dc, da, qscale)
        cn, att = [], []
        for b in range(bp):
            rows = slice(b * sp_len, (b + 1) * sp_len)
            cn.append(_conv_prompt(u[rows], *conv_w))
            att.append(_sb_prompt(qb[rows], kb[rows], vb[rows], sb_bias[l], hd))
        cn = cn[0] if bp == 1 else jnp.concatenate(cn)
        att = att[0] if bp == 1 else jnp.concatenate(att)
        x1, hb, qp3 = _merge(xp, cn, att, g0, g1, wco, wao, wo, norm2_g[l], wq)
        xp = _peer(x1, hb, qp3, keys_bf, u_bf, vt_bf, final_g)
        u3 = u.reshape(bp, sp_len, dc)
        buf = jnp.concatenate([jnp.zeros((bp, width - 1, dc), u.dtype), u3], axis=1)[:, -(width - 1):]
        outs["kp"].append(k.reshape(bp, sp_len, nh, hd))
        outs["vp"].append(v.reshape(bp, sp_len, nh, hd))
        outs["cp"].append(buf)

        u, k, v, qb, kb, vb, g0, g1 = _inproj(xs, norm1_g[l], w_in_bf, dc, da, qscale)
        ext = jnp.concatenate([state_conv[l], u.reshape(nb, tn, dc)], axis=1)
        cn = _conv_sample(jnp.transpose(ext, (1, 0, 2)), *conv_w, tn)
        cn = jnp.transpose(cn, (1, 0, 2)).reshape(nb * tn, dc)
        q_rep = jnp.repeat(qb.reshape(nb, tn, da), nh, axis=1)
        bias_rows = jnp.broadcast_to(jnp.tile(sb_bias[l], tn)[:, None],
                                     (tn * nh, pages_per_step * page)).astype(F32)
        new_t = lambda a: jnp.pad(jnp.transpose(a.reshape(nb, tn, da), (0, 2, 1)), ((0, 0), (0, 0), (0, page - tn)))
        att = _sb_sample(q_rep, bias_rows, new_t(kb), new_t(vb), cache_kt, cache_vt, l, page_table, tn,
                         pages_per_step)
        x1, hb, qp3 = _merge(xs, cn, att.reshape(nb * tn, da), g0, g1, wco, wao, wo, norm2_g[l], wq)
        xs = _peer(x1, hb, qp3, keys_bf, u_bf, vt_bf, final_g)
        outs["ks"].append(k.reshape(nb, tn, nh, hd))
        outs["vs"].append(v.reshape(nb, tn, nh, hd))
        outs["cs"].append(ext[:, -(width - 1):])

    return (xp.reshape(bp, sp_len, dm), xs.reshape(nb, tn, dm), jnp.stack(outs["kp"]), jnp.stack(outs["vp"]),
            jnp.stack(outs["cp"]), jnp.stack(outs["ks"]), jnp.stack(outs["vs"]), jnp.stack(outs["cs"]))
```
